```python
import math
import jax, jax.numpy as jnp
from jax import lax
import numpy as np

D_MODEL = 2048
BATCH = 4
SEQ = 4096
DEPTH = 1

HEAD_DIM = 128
N_HEADS_TOTAL = D_MODEL // HEAD_DIM
N_DIFF_HEADS = N_HEADS_TOTAL // 2
N_MOBA_HEADS = N_HEADS_TOTAL - N_DIFF_HEADS
DIFF_SUB = HEAD_DIM // 2
DIFF_W = N_DIFF_HEADS * HEAD_DIM
MOBA_W = N_MOBA_HEADS * HEAD_DIM
MIX_W = DIFF_W + MOBA_W
IN_COLS = 3 * DIFF_W + 3 * MOBA_W
ROPE_THETA = 500000.0
ROT_FRACTION = 4
MOBA_BLOCK = 256
MOBA_TOPK = 3
MOBA_Q_CHUNK = 32
DIFF_Q_CHUNK = 128
FFN_DIM = 5632
CONV_W = 3
LN_EPS = 1e-5
RMS_EPS = 1e-5
DEEPNORM_ALPHA = (2.0 * DEPTH) ** 0.25
DEEPNORM_BETA = (8.0 * DEPTH) ** -0.25
DIFF_SCALE = DIFF_SUB ** -0.5
MOBA_SCALE = HEAD_DIM ** -0.5

kernel_name = "hymba_diffattn_moba_convffn_deepnorm"


def rope_tables(seq, rot_dim):
    inv = 1.0 / (ROPE_THETA ** (jnp.arange(0, rot_dim, 2, dtype=jnp.float32) / rot_dim))
    pos = jnp.arange(seq, dtype=jnp.float32)
    ang = pos[:, None] * inv[None, :]
    return jnp.cos(ang), jnp.sin(ang)


def apply_partial_rope(x, cos, sin):
    half = cos.shape[-1]
    rot = 2 * half
    x1, x2, xp = x[..., :half], x[..., half:rot], x[..., rot:]
    c = cos.astype(x.dtype)
    s = sin.astype(x.dtype)
    return jnp.concatenate([x1 * c - x2 * s, x2 * c + x1 * s, xp], axis=-1)


def layer_norm(x, g, b):
    xf = x.astype(jnp.float32)
    mu = jnp.mean(xf, axis=-1, keepdims=True)
    var = jnp.mean(jnp.square(xf - mu), axis=-1, keepdims=True)
    y = (xf - mu) * lax.rsqrt(var + LN_EPS)
    return (y * g.astype(jnp.float32) + b.astype(jnp.float32)).astype(x.dtype)


def diff_attention(q, k, v, lam, lam_init, subln_g, cos, sin):
    B, S = q.shape[0], q.shape[1]
    H = N_DIFF_HEADS
    q = q.reshape(B, S, H, 2, DIFF_SUB).transpose(0, 2, 3, 1, 4)
    k = k.reshape(B, S, H, 2, DIFF_SUB).transpose(0, 2, 3, 1, 4)
    v = v.reshape(B, S, H, HEAD_DIM).transpose(0, 2, 1, 3)
    q = apply_partial_rope(q, cos, sin)
    k = apply_partial_rope(k, cos, sin)
    C = DIFF_Q_CHUNK
    nq = S // C
    q_blocks = jnp.moveaxis(q.reshape(B, H, 2, nq, C, DIFF_SUB), 3, 0)
    kpos = jnp.arange(S)

    def one_block(args):
        q_blk, i = args
        s = jnp.einsum('bhjcd,bhjkd->bhjck', q_blk, k,
                       preferred_element_type=jnp.float32) * DIFF_SCALE
        qpos = i * C + jnp.arange(C)
        s = jnp.where(kpos[None, :] <= qpos[:, None], s, -jnp.inf)
        p = jax.nn.softmax(s, axis=-1)
        a = p[:, :, 0] - lam * p[:, :, 1]
        return jnp.einsum('bhck,bhkd->bhcd', a.astype(v.dtype), v)

    o = lax.map(one_block, (q_blocks, jnp.arange(nq)))
    o = jnp.moveaxis(o, 0, 2).reshape(B, H, S, HEAD_DIM)
    of = o.astype(jnp.float32)
    of = of * lax.rsqrt(jnp.mean(jnp.square(of), axis=-1, keepdims=True) + RMS_EPS)
    of = of * subln_g.astype(jnp.float32) * (1.0 - lam_init)
    return of.astype(v.dtype).transpose(0, 2, 1, 3).reshape(B, S, DIFF_W)


def moba_attention(q, k, v, cos, sin):
    B, S = q.shape[0], q.shape[1]
    H = N_MOBA_HEADS
    q = q.reshape(B, S, H, HEAD_DIM).transpose(0, 2, 1, 3)
    k = k.reshape(B, S, H, HEAD_DIM).transpose(0, 2, 1, 3)
    v = v.reshape(B, S, H, HEAD_DIM).transpose(0, 2, 1, 3)
    q = apply_partial_rope(q, cos, sin)
    k = apply_partial_rope(k, cos, sin)
    NB = -(-S // MOBA_BLOCK)
    S_pad = NB * MOBA_BLOCK
    K_SEL = min(MOBA_TOPK, NB)
    pad = ((0, 0), (0, 0), (0, S_pad - S), (0, 0))
    k_pad = jnp.pad(k, pad)
    v_pad = jnp.pad(v, pad)
    kb = k_pad.reshape(B, H, NB, MOBA_BLOCK, HEAD_DIM)
    vb = v_pad.reshape(B, H, NB, MOBA_BLOCK, HEAD_DIM)
    counts = jnp.minimum(MOBA_BLOCK, S - jnp.arange(NB) * MOBA_BLOCK).astype(jnp.float32)
    kmean = jnp.sum(kb.astype(jnp.float32), axis=3) / counts[None, None, :, None]
    gate = jnp.einsum('bhsd,bhnd->bhsn', q.astype(jnp.float32), kmean)
    qblk = jnp.arange(S) // MOBA_BLOCK
    past = jnp.arange(NB)[None, :] < qblk[:, None]
    gate = jnp.where(past, gate, -jnp.inf)
    top_val, top_idx = lax.top_k(gate, K_SEL)
    top_valid = jnp.isfinite(top_val)

    C = MOBA_Q_CHUNK
    nc = S // C
    q_c = jnp.moveaxis(q.reshape(B, H, nc, C, HEAD_DIM), 2, 0)
    idx_c = jnp.moveaxis(top_idx.reshape(B, H, nc, C, K_SEL), 2, 0)
    val_c = jnp.moveaxis(top_valid.reshape(B, H, nc, C, K_SEL), 2, 0)
    bi = jnp.arange(B)[:, None, None, None]
    hi = jnp.arange(H)[None, :, None, None]

    def one_chunk(args):
        qc, ic, vc, i = args
        start = i * C
        blk_start = (start // MOBA_BLOCK) * MOBA_BLOCK
        k_own = lax.dynamic_slice_in_dim(k_pad, blk_start, MOBA_BLOCK, axis=2)
        v_own = lax.dynamic_slice_in_dim(v_pad, blk_start, MOBA_BLOCK, axis=2)
        qpos = start + jnp.arange(C)
        kpos = blk_start + jnp.arange(MOBA_BLOCK)
        s_own = jnp.einsum('bhcd,bhkd->bhck', qc, k_own,
                           preferred_element_type=jnp.float32) * MOBA_SCALE
        s_own = jnp.where(kpos[None, :] <= qpos[:, None], s_own, -jnp.inf)
        k_sel = kb[bi, hi, ic]
        v_sel = vb[bi, hi, ic]
        s_sel = jnp.einsum('bhcd,bhcjkd->bhcjk', qc, k_sel,
                           preferred_element_type=jnp.float32) * MOBA_SCALE
        s_sel = jnp.where(vc[..., None], s_sel, -jnp.inf)
        s_sel = s_sel.reshape(B, H, C, K_SEL * MOBA_BLOCK)
        p = jax.nn.softmax(jnp.concatenate([s_own, s_sel], axis=-1), axis=-1)
        p_own = p[..., :MOBA_BLOCK].astype(v.dtype)
        p_sel = p[..., MOBA_BLOCK:].reshape(B, H, C, K_SEL, MOBA_BLOCK).astype(v.dtype)
        return (jnp.einsum('bhck,bhkd->bhcd', p_own, v_own)
                + jnp.einsum('bhcjk,bhcjkd->bhcd', p_sel, v_sel))

    o = lax.map(one_chunk, (q_c, idx_c, val_c, jnp.arange(nc)))
    o = jnp.moveaxis(o, 0, 2).reshape(B, H, S, HEAD_DIM)
    return o.transpose(0, 2, 1, 3).reshape(B, S, MOBA_W)


def conv_ffn(h, w_up, conv_w, conv_b, w_down):
    u = h @ w_up
    g, val = u[..., :FFN_DIM], u[..., FFN_DIM:]
    S = h.shape[1]
    gp = jnp.pad(g, ((0, 0), (CONV_W - 1, 0), (0, 0)))
    gc = conv_b + sum(conv_w[j] * gp[:, j:j + S] for j in range(CONV_W))
    return (jax.nn.silu(gc) * val) @ w_down


def setup_inputs(seed: int = 0) -> dict:
    key = jax.random.key(seed)
    ks = jax.random.split(key, 16)
    f32 = jnp.float32
    x = jax.random.normal(ks[0], (BATCH, SEQ, D_MODEL), f32)
    w_in = jax.random.normal(ks[1], (DEPTH, D_MODEL, IN_COLS), f32) * D_MODEL ** -0.5
    col_scale = np.ones((IN_COLS,), np.float32)
    col_scale[2 * DIFF_W:3 * DIFF_W] = DEEPNORM_BETA
    col_scale[3 * DIFF_W + 2 * MOBA_W:] = DEEPNORM_BETA
    w_in = w_in * jnp.asarray(col_scale)
    lambda_q1 = jax.random.normal(ks[2], (DEPTH, DIFF_SUB), f32) * 0.1
    lambda_k1 = jax.random.normal(ks[3], (DEPTH, DIFF_SUB), f32) * 0.1
    lambda_q2 = jax.random.normal(ks[4], (DEPTH, DIFF_SUB), f32) * 0.1
    lambda_k2 = jax.random.normal(ks[5], (DEPTH, DIFF_SUB), f32) * 0.1
    subln_g = 1.0 + 0.02 * jax.random.normal(ks[6], (DEPTH, HEAD_DIM), f32)
    w_out = jax.random.normal(ks[7], (DEPTH, MIX_W, D_MODEL), f32) * (MIX_W ** -0.5 * DEEPNORM_BETA)
    ln1_g = 1.0 + 0.02 * jax.random.normal(ks[8], (DEPTH, D_MODEL), f32)
    ln1_b = 0.02 * jax.random.normal(ks[9], (DEPTH, D_MODEL), f32)
    w_up = jax.random.normal(ks[10], (DEPTH, D_MODEL, 2 * FFN_DIM), f32) * (D_MODEL ** -0.5 * DEEPNORM_BETA)
    conv_w = jax.random.normal(ks[11], (DEPTH, CONV_W, FFN_DIM), f32) * CONV_W ** -0.5
    conv_b = 0.02 * jax.random.normal(ks[12], (DEPTH, FFN_DIM), f32)
    w_down = jax.random.normal(ks[13], (DEPTH, FFN_DIM, D_MODEL), f32) * (FFN_DIM ** -0.5 * DEEPNORM_BETA)
    ln2_g = 1.0 + 0.02 * jax.random.normal(ks[14], (DEPTH, D_MODEL), f32)
    ln2_b = 0.02 * jax.random.normal(ks[15], (DEPTH, D_MODEL), f32)
    return {"x": x, "w_in": w_in, "lambda_q1": lambda_q1, "lambda_k1": lambda_k1,
            "lambda_q2": lambda_q2, "lambda_k2": lambda_k2, "subln_g": subln_g,
            "w_out": w_out, "ln1_g": ln1_g, "ln1_b": ln1_b, "w_up": w_up,
            "conv_w": conv_w, "conv_b": conv_b, "w_down": w_down,
            "ln2_g": ln2_g, "ln2_b": ln2_b}


def reference(x, w_in, lambda_q1, lambda_k1, lambda_q2, lambda_k2, subln_g, w_out,
              ln1_g, ln1_b, w_up, conv_w, conv_b, w_down, ln2_g, ln2_b):
    S = x.shape[1]
    cos_d, sin_d = rope_tables(S, DIFF_SUB // ROT_FRACTION)
    cos_m, sin_m = rope_tables(S, HEAD_DIM // ROT_FRACTION)
    h = x
    for l in range(DEPTH):
        lam_init = 0.8 - 0.6 * math.exp(-0.3 * l)
        lam = (jnp.exp(jnp.sum(lambda_q1[l].astype(jnp.float32) * lambda_k1[l].astype(jnp.float32)))
               - jnp.exp(jnp.sum(lambda_q2[l].astype(jnp.float32) * lambda_k2[l].astype(jnp.float32)))
               + lam_init)
        proj = h @ w_in[l]
        o0 = 0
        q_d = proj[..., o0:o0 + DIFF_W]; o0 += DIFF_W
        k_d = proj[..., o0:o0 + DIFF_W]; o0 += DIFF_W
        v_d = proj[..., o0:o0 + DIFF_W]; o0 += DIFF_W
        q_m = proj[..., o0:o0 + MOBA_W]; o0 += MOBA_W
        k_m = proj[..., o0:o0 + MOBA_W]; o0 += MOBA_W
        v_m = proj[..., o0:o0 + MOBA_W]
        a_out = diff_attention(q_d, k_d, v_d, lam, lam_init, subln_g[l], cos_d, sin_d)
        b_out = moba_attention(q_m, k_m, v_m, cos_m, sin_m)
        mix = jnp.concatenate([a_out, b_out], axis=-1) @ w_out[l]
        h = layer_norm(DEEPNORM_ALPHA * h + mix, ln1_g[l], ln1_b[l])
        f = conv_ffn(h, w_up[l], conv_w[l], conv_b[l], w_down[l])
        h = layer_norm(DEEPNORM_ALPHA * h + f, ln2_g[l], ln2_b[l])
    return h
```

```python
import functools
import math

import jax
import jax.numpy as jnp
from jax import lax
from jax.experimental import pallas as pl
from jax.experimental.pallas import tpu as pltpu

F32 = jnp.float32
BF16 = jnp.bfloat16

LANES = 128
HEAD_DIM = 128
DIFF_SUB = HEAD_DIM // 2
ROPE_THETA = 500000.0
ROT_FRACTION = 4
MOBA_BLOCK = 256
MOBA_TOPK = 3
CONV_W = 3
CONV_HALO = 8
LN_EPS = 1e-5
RMS_EPS = 1e-5
DEPTH = 1
DEEPNORM_ALPHA = (2.0 * DEPTH) ** 0.25
DIFF_SCALE = DIFF_SUB ** -0.5
MOBA_SCALE = HEAD_DIM ** -0.5
NEG_INF = float("-inf")

_NT = (((1,), (1,)), ((), ()))


def _cparams(n_axes, vmem_mib):
    return pltpu.CompilerParams(
        dimension_semantics=("arbitrary",) * n_axes,
        vmem_limit_bytes=vmem_mib * 1024 * 1024,
    )


def _rope_coeffs(seq, period, scale):
    rot = period // ROT_FRACTION
    half = rot // 2
    inv = 1.0 / (ROPE_THETA ** (jnp.arange(0, rot, 2, dtype=F32) / rot))
    pos = jnp.arange(seq, dtype=F32)
    ang = pos[:, None] * inv[None, :]
    cos, sin = jnp.cos(ang), jnp.sin(ang)
    z_half = jnp.zeros((seq, half), F32)
    z_rest = jnp.zeros((seq, period - rot), F32)
    c = jnp.concatenate([cos, cos, jnp.ones((seq, period - rot), F32)], axis=1)
    s1 = jnp.concatenate([-sin, z_half, z_rest], axis=1)
    s2 = jnp.concatenate([z_half, sin, z_rest], axis=1)
    tab = jnp.stack([c, s1, s2], axis=0) * scale
    return jnp.tile(tab, (1, 1, LANES // period))


def _proj_rope_kernel(x_ref, w_ref, tab_ref, o_ref, xb_ref, *, diff_tiles, moba_tiles, n_heads_blk):
    j = pl.program_id(1)

    @pl.when(j == 0)
    def _():
        xb_ref[...] = x_ref[...].astype(BF16)

    acc = jnp.dot(xb_ref[...], w_ref[...], preferred_element_type=F32)

    def rope(half):
        c, s1, s2 = tab_ref[0, 0], tab_ref[0, 1], tab_ref[0, 2]
        for hb in range(n_heads_blk):
            a = acc[:, hb * LANES:(hb + 1) * LANES]
            o = a * c + pltpu.roll(a, LANES - half, 1) * s1 + pltpu.roll(a, half, 1) * s2
            o_ref[:, hb * LANES:(hb + 1) * LANES] = o.astype(o_ref.dtype)

    is_diff = functools.reduce(jnp.logical_or, [j == t for t in diff_tiles])
    is_moba = functools.reduce(jnp.logical_or, [j == t for t in moba_tiles])

    @pl.when(is_diff)
    def _():
        rope(DIFF_SUB // ROT_FRACTION // 2)

    @pl.when(is_moba)
    def _():
        rope(HEAD_DIM // ROT_FRACTION // 2)

    @pl.when(jnp.logical_not(jnp.logical_or(is_diff, is_moba)))
    def _():
        o_ref[...] = acc.astype(o_ref.dtype)


def _proj_rope(x2, w_in, seq, diff_w, moba_w):
    m, d = x2.shape
    n = w_in.shape[1]
    bm = min(1024, seq)
    bn = 1024
    assert seq % bm == 0 and m % bm == 0 and diff_w % bn == 0 and moba_w % bn == 0
    dt, mt = diff_w // bn, moba_w // bn
    q_d_tiles = list(range(0, dt))
    k_d_tiles = list(range(dt, 2 * dt))
    q_m_tiles = list(range(3 * dt, 3 * dt + mt))
    k_m_tiles = list(range(3 * dt + mt, 3 * dt + 2 * mt))
    tabs = jnp.stack([
        _rope_coeffs(seq, DIFF_SUB, DIFF_SCALE),
        _rope_coeffs(seq, DIFF_SUB, 1.0),
        _rope_coeffs(seq, HEAD_DIM, MOBA_SCALE),
        _rope_coeffs(seq, HEAD_DIM, 1.0),
    ], axis=0)

    def kind(j):
        k = jnp.int32(0)
        for kk, tiles in enumerate([q_d_tiles, k_d_tiles, q_m_tiles, k_m_tiles]):
            for t in tiles:
                k = jnp.where(j == t, kk, k)
        return k

    spt = seq // bm
    kern = functools.partial(_proj_rope_kernel, diff_tiles=q_d_tiles + k_d_tiles,
                             moba_tiles=q_m_tiles + k_m_tiles, n_heads_blk=bn // LANES)
    return pl.pallas_call(
        kern,
        out_shape=jax.ShapeDtypeStruct((m, n), BF16),
        grid=(m // bm, n // bn),
        in_specs=[
            pl.BlockSpec((bm, d), lambda i, j: (i, 0)),
            pl.BlockSpec((d, bn), lambda i, j: (0, j)),
            pl.BlockSpec((1, 3, bm, LANES), lambda i, j: (kind(j), 0, i % spt, 0)),
        ],
        out_specs=pl.BlockSpec((bm, bn), lambda i, j: (i, j)),
        scratch_shapes=[pltpu.VMEM((bm, d), BF16)],
        compiler_params=_cparams(2, 56),
        name="proj_rope",
    )(x2, w_in, tabs)


def _lane_tile(a, width):
    reps = width // LANES
    return a if reps == 1 else jnp.concatenate([a] * reps, axis=1)


def _softmax_init(s, v, m_ref, l_ref, acc_ref):
    m = jnp.max(s, axis=1, keepdims=True)
    p = jnp.exp(s - m)
    m_ref[...] = jnp.broadcast_to(m, m_ref.shape)
    l_ref[...] = jnp.broadcast_to(jnp.sum(p, axis=1, keepdims=True), l_ref.shape)
    acc_ref[...] = jnp.dot(p.astype(v.dtype), v, preferred_element_type=F32)


def _softmax_step(s, v, m_ref, l_ref, acc_ref):
    m_prev = m_ref[...]
    m_new = jnp.maximum(m_prev, jnp.max(s, axis=1, keepdims=True))
    alpha = jnp.exp(m_prev - m_new)
    p = jnp.exp(s - _lane_tile(m_new, s.shape[1]))
    l_ref[...] = alpha * l_ref[...] + jnp.sum(p, axis=1, keepdims=True)
    acc_ref[...] = acc_ref[...] * _lane_tile(alpha, acc_ref.shape[1]) + jnp.dot(
        p.astype(v.dtype), v, preferred_element_type=F32)
    m_ref[...] = m_new


def _diff_attn_kernel(lq1_ref, lk1_ref, lq2_ref, lk2_ref, g_ref, q_ref, k_ref, v_ref, o_ref,
                      qs_ref, m_ref, l_ref, acc_ref, *, tq, lam_init):
    i = pl.program_id(2)
    q = q_ref[...]
    lane = lax.broadcasted_iota(jnp.int32, q.shape, 1)
    zero = jnp.zeros_like(q)
    qs_ref[0:tq, :] = jnp.where(lane < DIFF_SUB, q, zero)
    qs_ref[tq:2 * tq, :] = jnp.where(lane >= DIFF_SUB, q, zero)

    def scores(j):
        k = k_ref[pl.ds(pl.multiple_of(j * tq, tq), tq), :]
        return lax.dot_general(qs_ref[...], k, _NT, preferred_element_type=F32)

    def v_tile(j):
        return v_ref[pl.ds(pl.multiple_of(j * tq, tq), tq), :]

    s = scores(i)
    row = lax.broadcasted_iota(jnp.int32, s.shape, 0)
    col = lax.broadcasted_iota(jnp.int32, s.shape, 1)
    s = jnp.where(col <= (row & (tq - 1)), s, NEG_INF)
    _softmax_init(s, v_tile(i), m_ref, l_ref, acc_ref)

    def body(j, carry):
        _softmax_step(scores(j), v_tile(j), m_ref, l_ref, acc_ref)
        return carry

    lax.fori_loop(0, i, body, 0)

    lam = (jnp.exp(jnp.sum(lq1_ref[...] * lk1_ref[...])) - jnp.exp(jnp.sum(lq2_ref[...] * lk2_ref[...]))
           + lam_init)
    o1 = acc_ref[0:tq, :] / l_ref[0:tq, :]
    o2 = acc_ref[tq:2 * tq, :] / l_ref[tq:2 * tq, :]
    o = o1 - lam * o2
    o = o * lax.rsqrt(jnp.mean(jnp.square(o), axis=-1, keepdims=True) + RMS_EPS)
    o = o * g_ref[...] * (1.0 - lam_init)
    o_ref[...] = o.astype(o_ref.dtype)


def _diff_attn(proj, lam_params, subln_g, batch, seq, n_heads, q_col, k_col, v_col, lam_init):
    tq = min(256, seq)
    assert seq % tq == 0 and (tq & (tq - 1)) == 0
    nq = seq // tq
    small = lambda n: pl.BlockSpec((1, n), lambda b, h, i: (0, 0))
    kern = functools.partial(_diff_attn_kernel, tq=tq, lam_init=lam_init)
    return pl.pallas_call(
        kern,
        out_shape=jax.ShapeDtypeStruct((batch * seq, n_heads * HEAD_DIM), BF16),
        grid=(batch, n_heads, nq),
        in_specs=[small(DIFF_SUB)] * 4 + [
            small(HEAD_DIM),
            pl.BlockSpec((tq, HEAD_DIM), lambda b, h, i: (b * nq + i, q_col + h)),
            pl.BlockSpec((seq, HEAD_DIM), lambda b, h, i: (b, k_col + h)),
            pl.BlockSpec((seq, HEAD_DIM), lambda b, h, i: (b, v_col + h)),
        ],
        out_specs=pl.BlockSpec((tq, HEAD_DIM), lambda b, h, i: (b * nq + i, h)),
        scratch_shapes=[
            pltpu.VMEM((2 * tq, HEAD_DIM), BF16),
            pltpu.VMEM((2 * tq, LANES), F32),
            pltpu.VMEM((2 * tq, LANES), F32),
            pltpu.VMEM((2 * tq, HEAD_DIM), F32),
        ],
        compiler_params=_cparams(3, 32),
        name="diff_attn",
    )(*lam_params, subln_g, proj, proj, proj)


def _moba_attn_kernel(q_ref, k_ref, v_ref, o_ref, kmh_ref, kml_ref, sel_ref, m_ref, l_ref, acc_ref,
                      *, n_blocks):
    i = pl.program_id(2)
    t = MOBA_BLOCK

    @pl.when(i == 0)
    def _():
        kmean = jnp.sum(k_ref[...].astype(F32).reshape(n_blocks, t, HEAD_DIM), axis=1) * (1.0 / t)
        if n_blocks < LANES:
            kmean = jnp.concatenate([kmean, jnp.zeros((LANES - n_blocks, HEAD_DIM), F32)], axis=0)
        hi = kmean.astype(BF16)
        kmh_ref[...] = hi
        kml_ref[...] = (kmean - hi.astype(F32)).astype(BF16)

    q = q_ref[...]
    lane = lax.broadcasted_iota(jnp.int32, (t, LANES), 1)
    lane_f = lane.astype(F32)

    gate = (lax.dot_general(q, kmh_ref[...], _NT, preferred_element_type=F32)
            + lax.dot_general(q, kml_ref[...], _NT, preferred_element_type=F32))
    gate = jnp.where(lane < i, gate, NEG_INF)
    sel = jnp.zeros((t, LANES), F32)
    for _ in range(min(MOBA_TOPK, n_blocks)):
        mx = jnp.max(gate, axis=1, keepdims=True)
        cand = jnp.where(jnp.logical_and(gate == mx, gate > NEG_INF), lane_f, float(LANES))
        pick = lane_f == jnp.min(cand, axis=1, keepdims=True)
        sel = jnp.where(pick, 1.0, sel)
        gate = jnp.where(pick, NEG_INF, gate)
    sel_ref[...] = sel

    def kv(n):
        start = pl.multiple_of(n * t, t)
        return k_ref[pl.ds(start, t), :], v_ref[pl.ds(start, t), :]

    k_own, v_own = kv(i)
    s = lax.dot_general(q, k_own, _NT, preferred_element_type=F32)
    row = lax.broadcasted_iota(jnp.int32, s.shape, 0)
    col = lax.broadcasted_iota(jnp.int32, s.shape, 1)
    s = jnp.where(col <= row, s, NEG_INF)
    _softmax_init(s, v_own, m_ref, l_ref, acc_ref)

    def body(n, carry):
        k_n, v_n = kv(n)
        picked = jnp.sum(jnp.where(lane == n, sel_ref[...], 0.0), axis=1, keepdims=True) > 0.5
        s_n = lax.dot_general(q, k_n, _NT, preferred_element_type=F32)
        s_n = jnp.where(picked, s_n, NEG_INF)
        _softmax_step(s_n, v_n, m_ref, l_ref, acc_ref)
        return carry

    lax.fori_loop(0, i, body, 0)
    o_ref[...] = (acc_ref[...] / l_ref[...]).astype(o_ref.dtype)


def _moba_attn(proj, batch, seq, n_heads, q_col, k_col, v_col):
    t = MOBA_BLOCK
    assert seq % t == 0 and seq // t <= LANES
    nb = seq // t
    kern = functools.partial(_moba_attn_kernel, n_blocks=nb)
    return pl.pallas_call(
        kern,
        out_shape=jax.ShapeDtypeStruct((batch * seq, n_heads * HEAD_DIM), BF16),
        grid=(batch, n_heads, nb),
        in_specs=[
            pl.BlockSpec((t, HEAD_DIM), lambda b, h, i: (b * nb + i, q_col + h)),
            pl.BlockSpec((seq, HEAD_DIM), lambda b, h, i: (b, k_col + h)),
            pl.BlockSpec((seq, HEAD_DIM), lambda b, h, i: (b, v_col + h)),
        ],
        out_specs=pl.BlockSpec((t, HEAD_DIM), lambda b, h, i: (b * nb + i, h)),
        scratch_shapes=[
            pltpu.VMEM((LANES, HEAD_DIM), BF16),
            pltpu.VMEM((LANES, HEAD_DIM), BF16),
            pltpu.VMEM((t, LANES), F32),
            pltpu.VMEM((t, LANES), F32),
            pltpu.VMEM((t, LANES), F32),
            pltpu.VMEM((t, HEAD_DIM), F32),
        ],
        compiler_params=_cparams(3, 32),
        name="moba_attn",
    )(proj, proj, proj)


def _layer_norm(y, g, b):
    mu = jnp.mean(y, axis=-1, keepdims=True)
    yc = y - mu
    var = jnp.mean(jnp.square(yc), axis=-1, keepdims=True)
    return yc * lax.rsqrt(var + LN_EPS) * g + b


def _out_ln1_kernel(a_ref, b_ref, wa_ref, wb_ref, x_ref, g_ref, beta_ref, h_ref, hb_ref):
    mix = (jnp.dot(a_ref[...], wa_ref[...], preferred_element_type=F32)
           + jnp.dot(b_ref[...], wb_ref[...], preferred_element_type=F32))
    h = _layer_norm(DEEPNORM_ALPHA * x_ref[...] + mix, g_ref[...], beta_ref[...])
    h_ref[...] = h
    hb_ref[...] = h.astype(hb_ref.dtype)


def _out_ln1(a_out, b_out, w_out, x2, g, beta):
    m, d = x2.shape
    wa, wb = a_out.shape[1], b_out.shape[1]
    bm = 512
    assert m % bm == 0
    row = lambda w: pl.BlockSpec((bm, w), lambda i: (i, 0))
    vec = pl.BlockSpec((1, d), lambda i: (0, 0))
    return pl.pallas_call(
        _out_ln1_kernel,
        out_shape=(jax.ShapeDtypeStruct((m, d), F32), jax.ShapeDtypeStruct((m, d), BF16)),
        grid=(m // bm,),
        in_specs=[
            row(wa), row(wb),
            pl.BlockSpec((wa, d), lambda i: (0, 0)),
            pl.BlockSpec((wb, d), lambda i: (wa // wb, 0)),
            row(d), vec, vec,
        ],
        out_specs=(row(d), row(d)),
        compiler_params=_cparams(1, 56),
        name="out_ln1",
    )(a_out, b_out, w_out, w_out, x2, g, beta)


def _ffn_up_kernel(h_ref, wg_ref, wv_ref, cw_ref, cb_ref, o_ref, gbuf_ref, *, bm, tiles_per_seq):
    r = pl.program_id(1)
    h = h_ref[...]
    g = jnp.dot(h, wg_ref[...], preferred_element_type=F32)
    val = jnp.dot(h, wv_ref[...], preferred_element_type=F32)

    @pl.when(r % tiles_per_seq == 0)
    def _():
        gbuf_ref[0:CONV_HALO, :] = jnp.zeros((CONV_HALO, g.shape[1]), F32)

    gbuf_ref[CONV_HALO:CONV_HALO + bm, :] = g
    g1 = gbuf_ref[pl.ds(CONV_HALO - 1, bm), :]
    g2 = gbuf_ref[pl.ds(CONV_HALO - 2, bm), :]
    cw = cw_ref[...]
    gc = cb_ref[...] + (cw[0:1, :] * g2 + cw[1:2, :] * g1 + cw[2:3, :] * g)
    act = gc * jax.nn.sigmoid(gc) * val
    o_ref[...] = act.astype(o_ref.dtype)
    gbuf_ref[0:CONV_HALO, :] = gbuf_ref[bm:bm + CONV_HALO, :]


def _ffn_up(hb, w_up, conv_w, conv_b, seq):
    m, d = hb.shape
    f = conv_w.shape[1]
    bm = min(1024, seq)
    bn = 512
    assert seq % bm == 0 and f % bn == 0
    nct = f // bn
    kern = functools.partial(_ffn_up_kernel, bm=bm, tiles_per_seq=seq // bm)
    return pl.pallas_call(
        kern,
        out_shape=jax.ShapeDtypeStruct((m, f), BF16),
        grid=(nct, m // bm),
        in_specs=[
            pl.BlockSpec((bm, d), lambda c, r: (r, 0)),
            pl.BlockSpec((d, bn), lambda c, r: (0, c)),
            pl.BlockSpec((d, bn), lambda c, r: (0, nct + c)),
            pl.BlockSpec((CONV_W, bn), lambda c, r: (0, c)),
            pl.BlockSpec((1, bn), lambda c, r: (0, c)),
        ],
        out_specs=pl.BlockSpec((bm, bn), lambda c, r: (r, c)),
        scratch_shapes=[pltpu.VMEM((bm + CONV_HALO, bn), F32)],
        compiler_params=_cparams(2, 48),
        name="ffn_up",
    )(hb, w_up, w_up, conv_w, conv_b)


def _ffn_down_kernel(a_ref, w_ref, h_ref, g_ref, beta_ref, o_ref, acc_ref):
    k = pl.program_id(1)

    @pl.when(k == 0)
    def _():
        acc_ref[...] = jnp.zeros_like(acc_ref)

    acc_ref[...] += jnp.dot(a_ref[...], w_ref[...], preferred_element_type=F32)

    @pl.when(k == pl.num_programs(1) - 1)
    def _():
        o_ref[...] = _layer_norm(DEEPNORM_ALPHA * h_ref[...] + acc_ref[...], g_ref[...], beta_ref[...])


def _ffn_down(act, w_down, h1, g, beta):
    m, f = act.shape
    d = w_down.shape[1]
    bm, bk = 512, 512
    assert m % bm == 0 and f % bk == 0
    vec = pl.BlockSpec((1, d), lambda i, k: (0, 0))
    return pl.pallas_call(
        _ffn_down_kernel,
        out_shape=jax.ShapeDtypeStruct((m, d), F32),
        grid=(m // bm, f // bk),
        in_specs=[
            pl.BlockSpec((bm, bk), lambda i, k: (i, k)),
            pl.BlockSpec((bk, d), lambda i, k: (k, 0)),
            pl.BlockSpec((bm, d), lambda i, k: (i, 0)),
            vec, vec,
        ],
        out_specs=pl.BlockSpec((bm, d), lambda i, k: (i, 0)),
        scratch_shapes=[pltpu.VMEM((bm, d), F32)],
        compiler_params=_cparams(2, 48),
        name="ffn_down",
    )(act, w_down, h1, g, beta)


def kernel(x, w_in, lambda_q1, lambda_k1, lambda_q2, lambda_k2, subln_g, w_out, ln1_g, ln1_b, w_up,
           conv_w, conv_b, w_down, ln2_g, ln2_b):
    batch, seq, d_model = x.shape
    depth = w_in.shape[0]
    mix_w = w_out.shape[1]
    diff_w = moba_w = mix_w // 2
    n_dh, n_mh = diff_w // HEAD_DIM, moba_w // HEAD_DIM
    assert w_in.shape[2] == 3 * diff_w + 3 * moba_w

    h = x.reshape(batch * seq, d_model)
    for l in range(depth):
        lam_init = 0.8 - 0.6 * math.exp(-0.3 * l)
        proj = _proj_rope(h, w_in[l].astype(BF16), seq, diff_w, moba_w)
        lam_params = [p[l].reshape(1, DIFF_SUB) for p in (lambda_q1, lambda_k1, lambda_q2, lambda_k2)]
        a_out = _diff_attn(proj, lam_params, subln_g[l].reshape(1, HEAD_DIM), batch, seq, n_dh,
                           0, n_dh, 2 * n_dh, lam_init)
        b_out = _moba_attn(proj, batch, seq, n_mh, 3 * n_dh, 3 * n_dh + n_mh, 3 * n_dh + 2 * n_mh)
        h1, h1b = _out_ln1(a_out, b_out, w_out[l].astype(BF16), h,
                           ln1_g[l].reshape(1, d_model), ln1_b[l].reshape(1, d_model))
        act = _ffn_up(h1b, w_up[l].astype(BF16), conv_w[l], conv_b[l].reshape(1, -1), seq)
        h = _ffn_down(act, w_down[l].astype(BF16), h1, ln2_g[l].reshape(1, d_model),
                      ln2_b[l].reshape(1, d_model))
    return h.reshape(batch, seq, d_model)
```

```python
import functools
import math

import jax
import jax.numpy as jnp
from jax import lax
from jax.experimental import pallas as pl
from jax.experimental.pallas import tpu as pltpu

F32 = jnp.float32
BF16 = jnp.bfloat16

LANES = 128
BF16_SUBLANES = 16
HEAD_DIM = 128
VT_ROWS = HEAD_DIM + BF16_SUBLANES
LOG2E = 1.4426950408889634
DIFF_SUB = HEAD_DIM // 2
ROPE_THETA = 500000.0
ROT_FRACTION = 4
MOBA_BLOCK = 256
MOBA_TOPK = 3
CONV_W = 3
CONV_HALO = 8
LN_EPS = 1e-5
RMS_EPS = 1e-5
DEPTH = 1
DEEPNORM_ALPHA = (2.0 * DEPTH) ** 0.25
DIFF_SCALE = DIFF_SUB ** -0.5
MOBA_SCALE = HEAD_DIM ** -0.5
NEG_INF = float("-inf")

_NT = (((1,), (1,)), ((), ()))


def _cparams(n_axes, vmem_mib):
    return pltpu.CompilerParams(
        dimension_semantics=("arbitrary",) * n_axes,
        vmem_limit_bytes=vmem_mib * 1024 * 1024,
    )


def _rope_coeffs(seq, period, scale):
    rot = period // ROT_FRACTION
    half = rot // 2
    inv = 1.0 / (ROPE_THETA ** (jnp.arange(0, rot, 2, dtype=F32) / rot))
    pos = jnp.arange(seq, dtype=F32)
    ang = pos[:, None] * inv[None, :]
    cos, sin = jnp.cos(ang), jnp.sin(ang)
    z_half = jnp.zeros((seq, half), F32)
    z_rest = jnp.zeros((seq, period - rot), F32)
    c = jnp.concatenate([cos, cos, jnp.ones((seq, period - rot), F32)], axis=1)
    s1 = jnp.concatenate([-sin, z_half, z_rest], axis=1)
    s2 = jnp.concatenate([z_half, sin, z_rest], axis=1)
    tab = jnp.stack([c, s1, s2], axis=0) * scale
    return jnp.tile(tab, (1, 1, LANES // period))


def _proj_rope_kernel(x_ref, w_ref, tab_ref, o_ref, xb_ref, *, diff_tiles, moba_tiles, n_heads_blk):
    j = pl.program_id(1)

    @pl.when(j == 0)
    def _():
        xb_ref[...] = x_ref[...].astype(BF16)

    acc = jnp.dot(xb_ref[...], w_ref[...], preferred_element_type=F32)

    def rope(half):
        c, s1, s2 = tab_ref[0, 0], tab_ref[0, 1], tab_ref[0, 2]
        for hb in range(n_heads_blk):
            a = acc[:, hb * LANES:(hb + 1) * LANES]
            o = a * c + pltpu.roll(a, LANES - half, 1) * s1 + pltpu.roll(a, half, 1) * s2
            o_ref[:, hb * LANES:(hb + 1) * LANES] = o.astype(o_ref.dtype)

    is_diff = functools.reduce(jnp.logical_or, [j == t for t in diff_tiles])
    is_moba = functools.reduce(jnp.logical_or, [j == t for t in moba_tiles])

    @pl.when(is_diff)
    def _():
        rope(DIFF_SUB // ROT_FRACTION // 2)

    @pl.when(is_moba)
    def _():
        rope(HEAD_DIM // ROT_FRACTION // 2)

    @pl.when(jnp.logical_not(jnp.logical_or(is_diff, is_moba)))
    def _():
        o_ref[...] = acc.astype(o_ref.dtype)


def _proj_rope(x2, w_in, seq, diff_w, moba_w):
    m, d = x2.shape
    n = w_in.shape[1]
    bm = min(1024, seq)
    bn = 1024
    assert seq % bm == 0 and m % bm == 0 and diff_w % bn == 0 and moba_w % bn == 0
    dt, mt = diff_w // bn, moba_w // bn
    q_d_tiles = list(range(0, dt))
    k_d_tiles = list(range(dt, 2 * dt))
    q_m_tiles = list(range(3 * dt, 3 * dt + mt))
    k_m_tiles = list(range(3 * dt + mt, 3 * dt + 2 * mt))
    tabs = jnp.stack([
        _rope_coeffs(seq, DIFF_SUB, DIFF_SCALE * LOG2E),
        _rope_coeffs(seq, DIFF_SUB, 1.0),
        _rope_coeffs(seq, HEAD_DIM, MOBA_SCALE * LOG2E),
        _rope_coeffs(seq, HEAD_DIM, 1.0),
    ], axis=0)

    def kind(j):
        k = jnp.int32(0)
        for kk, tiles in enumerate([q_d_tiles, k_d_tiles, q_m_tiles, k_m_tiles]):
            for t in tiles:
                k = jnp.where(j == t, kk, k)
        return k

    spt = seq // bm
    kern = functools.partial(_proj_rope_kernel, diff_tiles=q_d_tiles + k_d_tiles,
                             moba_tiles=q_m_tiles + k_m_tiles, n_heads_blk=bn // LANES)
    return pl.pallas_call(
        kern,
        out_shape=jax.ShapeDtypeStruct((m, n), BF16),
        grid=(m // bm, n // bn),
        in_specs=[
            pl.BlockSpec((bm, d), lambda i, j: (i, 0)),
            pl.BlockSpec((d, bn), lambda i, j: (0, j)),
            pl.BlockSpec((1, 3, bm, LANES), lambda i, j: (kind(j), 0, i % spt, 0)),
        ],
        out_specs=pl.BlockSpec((bm, bn), lambda i, j: (i, j)),
        scratch_shapes=[pltpu.VMEM((bm, d), BF16)],
        compiler_params=_cparams(2, 56),
        name="proj_rope",
    )(x2, w_in, tabs)


def _softmax_step(s_t, v_t, m_ref, acc_ref):
    m_prev = m_ref[...]
    m_new = jnp.maximum(m_prev, jnp.max(s_t, axis=0, keepdims=True))
    alpha = jnp.exp2(m_prev - m_new)
    p = jnp.exp2(s_t - m_new)
    acc_ref[...] = acc_ref[...] * alpha + jnp.dot(v_t, p.astype(v_t.dtype), preferred_element_type=F32)
    m_ref[...] = m_new


def _flash_pipeline(n_full, j_diag, diag_scores, full_scores, vt_ref, s_ref, m_ref, acc_ref):
    m_ref[...] = jnp.full(m_ref.shape, NEG_INF, F32)
    acc_ref[...] = jnp.zeros(acc_ref.shape, F32)
    s_ref[0] = diag_scores()
    last_full = jnp.maximum(n_full - 1, 0)

    def pair(p, carry):
        s_ref[1] = full_scores(2 * p)
        _softmax_step(s_ref[0], vt_ref[jnp.where(p == 0, j_diag, 2 * p - 1)], m_ref, acc_ref)
        s_ref[0] = full_scores(jnp.minimum(2 * p + 1, last_full))
        _softmax_step(s_ref[1], vt_ref[2 * p], m_ref, acc_ref)
        return carry

    lax.fori_loop(0, (n_full + 1) // 2, pair, 0)

    @pl.when((n_full + 1) % 2 == 1)
    def _():
        _softmax_step(s_ref[0], vt_ref[jnp.where(n_full == 0, j_diag, n_full - 1)], m_ref, acc_ref)


def _store_v_transposed(v_ref, vt_ref, tk):
    pad = VT_ROWS - HEAD_DIM
    ones_row = (lax.broadcasted_iota(jnp.int32, (pad, tk), 0) == 0).astype(vt_ref.dtype)
    for j in range(vt_ref.shape[0]):
        vt_ref[j, 0:HEAD_DIM, :] = v_ref[j * tk:(j + 1) * tk, :].T
        vt_ref[j, HEAD_DIM:VT_ROWS, :] = ones_row


def _normalized(acc_ref):
    return acc_ref[0:HEAD_DIM, :] / acc_ref[HEAD_DIM:HEAD_DIM + 1, :]


def _diff_attn_kernel(lq1_ref, lk1_ref, lq2_ref, lk2_ref, g_ref, q_ref, k_ref, v_ref, o_ref,
                      qs_ref, vt_ref, s_ref, m_ref, acc_ref, *, tq, tk, lam_init):
    i = pl.program_id(2)

    @pl.when(i == 0)
    def _():
        _store_v_transposed(v_ref, vt_ref, tk)

    q = q_ref[...]
    lane = lax.broadcasted_iota(jnp.int32, q.shape, 1)
    zero = jnp.zeros_like(q)
    qs_ref[0:tq, :] = jnp.where(lane < DIFF_SUB, q, zero)
    qs_ref[tq:2 * tq, :] = jnp.where(lane >= DIFF_SUB, q, zero)

    def scores(j):
        k = k_ref[pl.ds(pl.multiple_of(j * tk, tk), tk), :]
        return lax.dot_general(k, qs_ref[...], _NT, preferred_element_type=F32)

    j_diag = (i * tq) // tk

    def diag_scores():
        s_t = scores(j_diag)
        kpos = j_diag * tk + lax.broadcasted_iota(jnp.int32, s_t.shape, 0)
        qpos = i * tq + (lax.broadcasted_iota(jnp.int32, s_t.shape, 1) & (tq - 1))
        return jnp.where(kpos <= qpos, s_t, NEG_INF)

    _flash_pipeline(j_diag, j_diag, diag_scores, scores, vt_ref, s_ref, m_ref, acc_ref)

    lam =(jnp.exp(jnp.sum(lq1_ref[...] * lk1_ref[...])) - jnp.exp(jnp.sum(lq2_ref[...] * lk2_ref[...]))
           + lam_init)
    o_t = _normalized(acc_ref)
    o_t = o_t[:, 0:tq] - lam * o_t[:, tq:2 * tq]
    o_t = o_t * lax.rsqrt(jnp.mean(jnp.square(o_t), axis=0, keepdims=True) + RMS_EPS)
    o_t = o_t * g_ref[...] * (1.0 - lam_init)
    o_ref[...] = o_t.T.astype(o_ref.dtype)


def _diff_attn(proj, lam_params, subln_g, batch, seq, n_heads, q_col, k_col, v_col, lam_init):
    tq = min(512, seq)
    tk = min(512, seq)
    assert seq % tk == 0 and tk % tq == 0 and (tq & (tq - 1)) == 0
    nq = seq // tq
    small = lambda n: pl.BlockSpec((1, n), lambda b, h, i: (0, 0))
    kern = functools.partial(_diff_attn_kernel, tq=tq, tk=tk, lam_init=lam_init)
    return pl.pallas_call(
        kern,
        out_shape=jax.ShapeDtypeStruct((batch * seq, n_heads * HEAD_DIM), BF16),
        grid=(batch, n_heads, nq),
        in_specs=[small(DIFF_SUB)] * 4 + [
            pl.BlockSpec((HEAD_DIM, 1), lambda b, h, i: (0, 0)),
            pl.BlockSpec((tq, HEAD_DIM), lambda b, h, i: (b * nq + i, q_col + h)),
            pl.BlockSpec((seq, HEAD_DIM), lambda b, h, i: (b, k_col + h)),
            pl.BlockSpec((seq, HEAD_DIM), lambda b, h, i: (b, v_col + h)),
        ],
        out_specs=pl.BlockSpec((tq, HEAD_DIM), lambda b, h, i: (b * nq + i, h)),
        scratch_shapes=[
            pltpu.VMEM((2 * tq, HEAD_DIM), BF16),
            pltpu.VMEM((seq // tk, VT_ROWS, tk), BF16),
            pltpu.VMEM((2, tk, 2 * tq), F32),
            pltpu.VMEM((1, 2 * tq), F32),
            pltpu.VMEM((VT_ROWS, 2 * tq), F32),
        ],
        compiler_params=_cparams(3, 32),
        name="diff_attn",
    )(*lam_params, subln_g, proj, proj, proj)


def _moba_attn_kernel(q_ref, k_ref, v_ref, o_ref, kmh_ref, kml_ref, vt_ref, sel_ref, s_ref, m_ref, acc_ref,
                      *, n_blocks, t):
    i = pl.program_id(2)
    blk = MOBA_BLOCK
    per = t // blk
    nbp = kmh_ref.shape[0]

    @pl.when(i == 0)
    def _():
        _store_v_transposed(v_ref, vt_ref, t)
        kmean = jnp.sum(k_ref[...].astype(F32).reshape(n_blocks, blk, HEAD_DIM), axis=1) * (1.0 / blk)
        if n_blocks < nbp:
            kmean = jnp.concatenate([kmean, jnp.zeros((nbp - n_blocks, HEAD_DIM), F32)], axis=0)
        hi = kmean.astype(BF16)
        kmh_ref[...] = hi
        kml_ref[...] = (kmean - hi.astype(F32)).astype(BF16)

    q = q_ref[...]
    gate = (lax.dot_general(kmh_ref[...], q, _NT, preferred_element_type=F32)
            + lax.dot_general(kml_ref[...], q, _NT, preferred_element_type=F32))
    blk_id = lax.broadcasted_iota(jnp.int32, (nbp, t), 0)
    q_blk = i * per + lax.broadcasted_iota(jnp.int32, (nbp, t), 1) // blk
    blk_f = blk_id.astype(F32)
    gate = jnp.where(blk_id < q_blk, gate, NEG_INF)
    sel = jnp.zeros((nbp, t), F32)
    for _ in range(min(MOBA_TOPK, n_blocks)):
        mx = jnp.max(gate, axis=0, keepdims=True)
        cand = jnp.where(jnp.logical_and(gate == mx, gate > NEG_INF), blk_f, float(nbp))
        pick = blk_f == jnp.min(cand, axis=0, keepdims=True)
        sel = jnp.where(pick, 1.0, sel)
        gate = jnp.where(pick, NEG_INF, gate)
    sel_ref[...] = sel

    def scores(g):
        k = k_ref[pl.ds(pl.multiple_of(g * t, t), t), :]
        return lax.dot_general(k, q, _NT, preferred_element_type=F32)

    def picked(g):
        rows = [jnp.broadcast_to(sel_ref[pl.ds(g * per + c, 1), :] > 0.5, (blk, t)) for c in range(per)]
        return rows[0] if per == 1 else jnp.concatenate(rows, axis=0)

    def past_scores(g):
        return jnp.where(picked(g), scores(g), NEG_INF)

    def diag_scores():
        r = lax.broadcasted_iota(jnp.int32, (t, t), 0)
        c = lax.broadcasted_iota(jnp.int32, (t, t), 1)
        own_causal = jnp.logical_and(r // blk == c // blk, r <= c)
        return jnp.where(jnp.logical_or(own_causal, picked(i)), scores(i), NEG_INF)

    _flash_pipeline(i, i, diag_scores, past_scores, vt_ref, s_ref, m_ref, acc_ref)
    o_ref[...] = _normalized(acc_ref).T.astype(o_ref.dtype)


def _moba_attn(proj, batch, seq, n_heads, q_col, k_col, v_col):
    t = min(2 * MOBA_BLOCK, seq)
    assert seq % t == 0 and t % MOBA_BLOCK == 0
    nb = seq // MOBA_BLOCK
    nbp = -(-nb // BF16_SUBLANES) * BF16_SUBLANES
    nt = seq // t
    kern = functools.partial(_moba_attn_kernel, n_blocks=nb, t=t)
    return pl.pallas_call(
        kern,
        out_shape=jax.ShapeDtypeStruct((batch * seq, n_heads * HEAD_DIM), BF16),
        grid=(batch, n_heads, nt),
        in_specs=[
            pl.BlockSpec((t, HEAD_DIM), lambda b, h, i: (b * nt + i, q_col + h)),
            pl.BlockSpec((seq, HEAD_DIM), lambda b, h, i: (b, k_col + h)),
            pl.BlockSpec((seq, HEAD_DIM), lambda b, h, i: (b, v_col + h)),
        ],
        out_specs=pl.BlockSpec((t, HEAD_DIM), lambda b, h, i: (b * nt + i, h)),
        scratch_shapes=[
            pltpu.VMEM((nbp, HEAD_DIM), BF16),
            pltpu.VMEM((nbp, HEAD_DIM), BF16),
            pltpu.VMEM((nt, VT_ROWS, t), BF16),
            pltpu.VMEM((nbp, t), F32),
            pltpu.VMEM((2, t, t), F32),
            pltpu.VMEM((1, t), F32),
            pltpu.VMEM((VT_ROWS, t), F32),
        ],
        compiler_params=_cparams(3, 32),
        name="moba_attn",
    )(proj, proj, proj)


def _layer_norm(y, g, b):
    mu = jnp.mean(y, axis=-1, keepdims=True)
    yc = y - mu
    var = jnp.mean(jnp.square(yc), axis=-1, keepdims=True)
    return yc * lax.rsqrt(var + LN_EPS) * g + b


def _out_ln1_kernel(a_ref, b_ref, wa_ref, wb_ref, x_ref, g_ref, beta_ref, h_ref, hb_ref):
    mix = (jnp.dot(a_ref[...], wa_ref[...], preferred_element_type=F32)
           + jnp.dot(b_ref[...], wb_ref[...], preferred_element_type=F32))
    h = _layer_norm(DEEPNORM_ALPHA * x_ref[...] + mix, g_ref[...], beta_ref[...])
    h_ref[...] = h
    hb_ref[...] = h.astype(hb_ref.dtype)


def _out_ln1(a_out, b_out, w_out, x2, g, beta):
    m, d = x2.shape
    wa, wb = a_out.shape[1], b_out.shape[1]
    bm = 512
    assert m % bm == 0
    row = lambda w: pl.BlockSpec((bm, w), lambda i: (i, 0))
    vec = pl.BlockSpec((1, d), lambda i: (0, 0))
    return pl.pallas_call(
        _out_ln1_kernel,
        out_shape=(jax.ShapeDtypeStruct((m, d), F32), jax.ShapeDtypeStruct((m, d), BF16)),
        grid=(m // bm,),
        in_specs=[
            row(wa), row(wb),
            pl.BlockSpec((wa, d), lambda i: (0, 0)),
            pl.BlockSpec((wb, d), lambda i: (wa // wb, 0)),
            row(d), vec, vec,
        ],
        out_specs=(row(d), row(d)),
        compiler_params=_cparams(1, 56),
        name="out_ln1",
    )(a_out, b_out, w_out, w_out, x2, g, beta)


def _ffn_up_kernel(h_ref, wg_ref, wv_ref, cw_ref, cb_ref, o_ref, gbuf_ref, *, bm, tiles_per_seq):
    r = pl.program_id(1)
    h = h_ref[...]
    g = jnp.dot(h, wg_ref[...], preferred_element_type=F32)
    val = jnp.dot(h, wv_ref[...], preferred_element_type=F32)

    @pl.when(r % tiles_per_seq == 0)
    def _():
        gbuf_ref[0:CONV_HALO, :] = jnp.zeros((CONV_HALO, g.shape[1]), F32)

    gbuf_ref[CONV_HALO:CONV_HALO + bm, :] = g
    g1 = gbuf_ref[pl.ds(CONV_HALO - 1, bm), :]
    g2 = gbuf_ref[pl.ds(CONV_HALO - 2, bm), :]
    cw = cw_ref[...]
    gc = cb_ref[...] + (cw[0:1, :] * g2 + cw[1:2, :] * g1 + cw[2:3, :] * g)
    act = gc * jax.nn.sigmoid(gc) * val
    o_ref[...] = act.astype(o_ref.dtype)
    gbuf_ref[0:CONV_HALO, :] = gbuf_ref[bm:bm + CONV_HALO, :]


def _ffn_up(hb, w_up, conv_w, conv_b, seq):
    m, d = hb.shape
    f = conv_w.shape[1]
    bm = min(1024, seq)
    bn = 512
    assert seq % bm == 0 and f % bn == 0
    nct = f // bn
    kern = functools.partial(_ffn_up_kernel, bm=bm, tiles_per_seq=seq // bm)
    return pl.pallas_call(
        kern,
        out_shape=jax.ShapeDtypeStruct((m, f), BF16),
        grid=(nct, m // bm),
        in_specs=[
            pl.BlockSpec((bm, d), lambda c, r: (r, 0)),
            pl.BlockSpec((d, bn), lambda c, r: (0, c)),
            pl.BlockSpec((d, bn), lambda c, r: (0, nct + c)),
            pl.BlockSpec((CONV_W, bn), lambda c, r: (0, c)),
            pl.BlockSpec((1, bn), lambda c, r: (0, c)),
        ],
        out_specs=pl.BlockSpec((bm, bn), lambda c, r: (r, c)),
        scratch_shapes=[pltpu.VMEM((bm + CONV_HALO, bn), F32)],
        compiler_params=_cparams(2, 48),
        name="ffn_up",
    )(hb, w_up, w_up, conv_w, conv_b)


def _ffn_down_kernel(a_ref, w_ref, h_ref, g_ref, beta_ref, o_ref, acc_ref):
    k = pl.program_id(1)

    @pl.when(k == 0)
    def _():
        acc_ref[...] = jnp.zeros_like(acc_ref)

    acc_ref[...] += jnp.dot(a_ref[...], w_ref[...], preferred_element_type=F32)

    @pl.when(k == pl.num_programs(1) - 1)
    def _():
        o_ref[...] = _layer_norm(DEEPNORM_ALPHA * h_ref[...] + acc_ref[...], g_ref[...], beta_ref[...])


def _ffn_down(act, w_down, h1, g, beta):
    m, f = act.shape
    d = w_down.shape[1]
    bm, bk = 512, 512
    assert m % bm == 0 and f % bk == 0
    vec = pl.BlockSpec((1, d), lambda i, k: (0, 0))
    return pl.pallas_call(
        _ffn_down_kernel,
        out_shape=jax.ShapeDtypeStruct((m, d), F32),
        grid=(m // bm, f // bk),
        in_specs=[
            pl.BlockSpec((bm, bk), lambda i, k: (i, k)),
            pl.BlockSpec((bk, d), lambda i, k: (k, 0)),
            pl.BlockSpec((bm, d), lambda i, k: (i, 0)),
            vec, vec,
        ],
        out_specs=pl.BlockSpec((bm, d), lambda i, k: (i, 0)),
        scratch_shapes=[pltpu.VMEM((bm, d), F32)],
        compiler_params=_cparams(2, 48),
        name="ffn_down",
    )(act, w_down, h1, g, beta)


def kernel(x, w_in, lambda_q1, lambda_k1, lambda_q2, lambda_k2, subln_g, w_out, ln1_g, ln1_b, w_up,
           conv_w, conv_b, w_down, ln2_g, ln2_b):
    batch, seq, d_model = x.shape
    depth = w_in.shape[0]
    mix_w = w_out.shape[1]
    diff_w = moba_w = mix_w // 2
    n_dh, n_mh = diff_w // HEAD_DIM, moba_w // HEAD_DIM
    assert w_in.shape[2] == 3 * diff_w + 3 * moba_w

    h = x.reshape(batch * seq, d_model)
    for l in range(depth):
        lam_init = 0.8 - 0.6 * math.exp(-0.3 * l)
        proj = _proj_rope(h, w_in[l].astype(BF16), seq, diff_w, moba_w)
        lam_params = [p[l].reshape(1, DIFF_SUB) for p in (lambda_q1, lambda_k1, lambda_q2, lambda_k2)]
        a_out = _diff_attn(proj, lam_params, subln_g[l].reshape(HEAD_DIM, 1), batch, seq, n_dh,
                           0, n_dh, 2 * n_dh, lam_init)
        b_out = _moba_attn(proj, batch, seq, n_mh, 3 * n_dh, 3 * n_dh + n_mh, 3 * n_dh + 2 * n_mh)
        h1, h1b = _out_ln1(a_out, b_out, w_out[l].astype(BF16), h,
                           ln1_g[l].reshape(1, d_model), ln1_b[l].reshape(1, d_model))
        act = _ffn_up(h1b, w_up[l].astype(BF16), conv_w[l], conv_b[l].reshape(1, -1), seq)
        h = _ffn_down(act, w_down[l].astype(BF16), h1, ln2_g[l].reshape(1, d_model),
                      ln2_b[l].reshape(1, d_model))
    return h.reshape(batch, seq, d_model)
```

```python
import functools
import math

import jax
import jax.numpy as jnp
from jax import lax
from jax.experimental import pallas as pl
from jax.experimental.pallas import tpu as pltpu

F32 = jnp.float32
BF16 = jnp.bfloat16

LANES = 128
BF16_SUBLANES = 16
HEAD_DIM = 128
VT_ROWS = HEAD_DIM + BF16_SUBLANES
LOG2E = 1.4426950408889634
DIFF_SUB = HEAD_DIM // 2
ROPE_THETA = 500000.0
ROT_FRACTION = 4
MOBA_BLOCK = 256
MOBA_TOPK = 3
CONV_W = 3
CONV_HALO = 8
ROW_CHUNK = 256
LN_EPS = 1e-5
RMS_EPS = 1e-5
DEPTH = 1
DEEPNORM_ALPHA = (2.0 * DEPTH) ** 0.25
DIFF_SCALE = DIFF_SUB ** -0.5
MOBA_SCALE = HEAD_DIM ** -0.5
NEG_INF = float("-inf")

_NT = (((1,), (1,)), ((), ()))


def _cparams(n_axes, vmem_mib):
    return pltpu.CompilerParams(
        dimension_semantics=("arbitrary",) * n_axes,
        vmem_limit_bytes=vmem_mib * 1024 * 1024,
    )


def _rope_coeffs(seq, period):
    rot = period // ROT_FRACTION
    half = rot // 2
    inv = 1.0 / (ROPE_THETA ** (jnp.arange(0, rot, 2, dtype=F32) / rot))
    pos = jnp.arange(seq, dtype=F32)
    ang = pos[:, None] * inv[None, :]
    cos, sin = jnp.cos(ang), jnp.sin(ang)
    z_half = jnp.zeros((seq, half), F32)
    z_rest = jnp.zeros((seq, period - rot), F32)
    c = jnp.concatenate([cos, cos, jnp.ones((seq, period - rot), F32)], axis=1)
    s1 = jnp.concatenate([-sin, z_half, z_rest], axis=1)
    s2 = jnp.concatenate([z_half, sin, z_rest], axis=1)
    return jnp.tile(jnp.stack([c, s1, s2], axis=0), (1, 1, LANES // period))


def _proj_rope_kernel(x_ref, w_ref, tab_ref, o_ref, xb_ref, *, diff_tiles, moba_tiles, q_scales,
                      n_heads_blk):
    j = pl.program_id(1)
    scale = jnp.float32(1.0)
    for t, s in q_scales:
        scale = jnp.where(j == t, jnp.float32(s), scale)

    def tile(half, first):
        for ch in range(x_ref.shape[0] // ROW_CHUNK):
            rows = slice(ch * ROW_CHUNK, (ch + 1) * ROW_CHUNK)
            if first:
                xb_ref[rows, :] = x_ref[rows, :].astype(BF16)
            acc = jnp.dot(xb_ref[rows, :], w_ref[...], preferred_element_type=F32)
            if half is None:
                o_ref[rows, :] = acc.astype(o_ref.dtype)
                continue
            c, s1, s2 = tab_ref[0, 0, rows, :], tab_ref[0, 1, rows, :], tab_ref[0, 2, rows, :]
            for hb in range(n_heads_blk):
                a = acc[:, hb * LANES:(hb + 1) * LANES]
                o = a * c + pltpu.roll(a, LANES - half, 1) * s1 + pltpu.roll(a, half, 1) * s2
                o_ref[rows, hb * LANES:(hb + 1) * LANES] = (o * scale).astype(o_ref.dtype)

    assert diff_tiles[0] == 0
    is_diff = functools.reduce(jnp.logical_or, [j == t for t in diff_tiles[1:]])
    is_moba = functools.reduce(jnp.logical_or, [j == t for t in moba_tiles])
    diff_half = DIFF_SUB // ROT_FRACTION // 2
    moba_half = HEAD_DIM // ROT_FRACTION // 2

    @pl.when(j == 0)
    def _():
        tile(diff_half, True)

    @pl.when(is_diff)
    def _():
        tile(diff_half, False)

    @pl.when(is_moba)
    def _():
        tile(moba_half, False)

    @pl.when(jnp.logical_not(jnp.logical_or(j == 0, jnp.logical_or(is_diff, is_moba))))
    def _():
        tile(None, False)


def _proj_rope(x2, w_in, seq, diff_w, moba_w):
    m, d = x2.shape
    n = w_in.shape[1]
    bm = min(1024, seq)
    bn = 1024
    assert seq % bm == 0 and m % bm == 0 and diff_w % bn == 0 and moba_w % bn == 0
    dt, mt = diff_w // bn, moba_w // bn
    q_d_tiles = list(range(0, dt))
    k_d_tiles = list(range(dt, 2 * dt))
    q_m_tiles = list(range(3 * dt, 3 * dt + mt))
    k_m_tiles = list(range(3 * dt + mt, 3 * dt + 2 * mt))
    tabs = jnp.stack([_rope_coeffs(seq, DIFF_SUB), _rope_coeffs(seq, HEAD_DIM)], axis=0)

    def kind(j):
        k = jnp.int32(0)
        for t in q_m_tiles + k_m_tiles:
            k = jnp.where(j == t, 1, k)
        return k

    spt = seq // bm
    q_scales = ([(t, DIFF_SCALE * LOG2E) for t in q_d_tiles] + [(t, MOBA_SCALE * LOG2E) for t in q_m_tiles])
    kern = functools.partial(_proj_rope_kernel, diff_tiles=q_d_tiles + k_d_tiles,
                             moba_tiles=q_m_tiles + k_m_tiles, q_scales=q_scales, n_heads_blk=bn // LANES)
    return pl.pallas_call(
        kern,
        out_shape=jax.ShapeDtypeStruct((m, n), BF16),
        grid=(m // bm, n // bn),
        in_specs=[
            pl.BlockSpec((bm, d), lambda i, j: (i, 0)),
            pl.BlockSpec((d, bn), lambda i, j: (0, j)),
            pl.BlockSpec((1, 3, bm, LANES), lambda i, j: (kind(j), 0, i % spt, 0)),
        ],
        out_specs=pl.BlockSpec((bm, bn), lambda i, j: (i, j)),
        scratch_shapes=[pltpu.VMEM((bm, d), BF16)],
        compiler_params=_cparams(2, 56),
        name="proj_rope",
    )(x2, w_in, tabs)


def _softmax_step(s_t, v_t, m_ref, acc_ref):
    m_prev = m_ref[...]
    m_new = jnp.maximum(m_prev, jnp.max(s_t, axis=0, keepdims=True))
    alpha = jnp.exp2(m_prev - m_new)
    p = jnp.exp2(s_t - m_new)
    acc_ref[...] = acc_ref[...] * alpha + jnp.dot(v_t, p.astype(v_t.dtype), preferred_element_type=F32)
    m_ref[...] = m_new


def _flash_pipeline(n_full, j_diag, diag_scores, full_scores, vt_ref, s_ref, m_ref, acc_ref):
    m_ref[...] = jnp.full(m_ref.shape, NEG_INF, F32)
    acc_ref[...] = jnp.zeros(acc_ref.shape, F32)
    s_ref[0] = diag_scores()
    last_full = jnp.maximum(n_full - 1, 0)

    def pair(p, carry):
        s_ref[1] = full_scores(2 * p)
        _softmax_step(s_ref[0], vt_ref[jnp.where(p == 0, j_diag, 2 * p - 1)], m_ref, acc_ref)
        s_ref[0] = full_scores(jnp.minimum(2 * p + 1, last_full))
        _softmax_step(s_ref[1], vt_ref[2 * p], m_ref, acc_ref)
        return carry

    lax.fori_loop(0, (n_full + 1) // 2, pair, 0)

    @pl.when((n_full + 1) % 2 == 1)
    def _():
        _softmax_step(s_ref[0], vt_ref[jnp.where(n_full == 0, j_diag, n_full - 1)], m_ref, acc_ref)


def _store_v_transposed(v_ref, vt_ref, tk):
    pad = VT_ROWS - HEAD_DIM
    ones_row = (lax.broadcasted_iota(jnp.int32, (pad, tk), 0) == 0).astype(vt_ref.dtype)
    for j in range(vt_ref.shape[0]):
        vt_ref[j, 0:HEAD_DIM, :] = v_ref[j * tk:(j + 1) * tk, :].T
        vt_ref[j, HEAD_DIM:VT_ROWS, :] = ones_row


def _normalized(acc_ref):
    return acc_ref[0:HEAD_DIM, :] / acc_ref[HEAD_DIM:HEAD_DIM + 1, :]


def _diff_attn_kernel(lq1_ref, lk1_ref, lq2_ref, lk2_ref, g_ref, q_ref, k_ref, v_ref, o_ref,
                      qs_ref, vt_ref, s_ref, m_ref, acc_ref, *, tq, tk, lam_init):
    i = pl.program_id(2)

    @pl.when(i == 0)
    def _():
        _store_v_transposed(v_ref, vt_ref, tk)

    q = q_ref[...]
    lane = lax.broadcasted_iota(jnp.int32, q.shape, 1)
    zero = jnp.zeros_like(q)
    qs_ref[0:tq, :] = jnp.where(lane < DIFF_SUB, q, zero)
    qs_ref[tq:2 * tq, :] = jnp.where(lane >= DIFF_SUB, q, zero)

    def scores(j):
        k = k_ref[pl.ds(pl.multiple_of(j * tk, tk), tk), :]
        return lax.dot_general(k, qs_ref[...], _NT, preferred_element_type=F32)

    j_diag = (i * tq) // tk

    def diag_scores():
        s_t = scores(j_diag)
        kpos = j_diag * tk + lax.broadcasted_iota(jnp.int32, s_t.shape, 0)
        qpos = i * tq + (lax.broadcasted_iota(jnp.int32, s_t.shape, 1) & (tq - 1))
        return jnp.where(kpos <= qpos, s_t, NEG_INF)

    _flash_pipeline(j_diag, j_diag, diag_scores, scores, vt_ref, s_ref, m_ref, acc_ref)

    lam =(jnp.exp(jnp.sum(lq1_ref[...] * lk1_ref[...])) - jnp.exp(jnp.sum(lq2_ref[...] * lk2_ref[...]))
           + lam_init)
    o_t = _normalized(acc_ref)
    o_t = o_t[:, 0:tq] - lam * o_t[:, tq:2 * tq]
    o_t = o_t * lax.rsqrt(jnp.mean(jnp.square(o_t), axis=0, keepdims=True) + RMS_EPS)
    o_t = o_t * g_ref[...] * (1.0 - lam_init)
    o_ref[...] = o_t.T.astype(o_ref.dtype)


def _diff_attn(proj, lam_params, subln_g, batch, seq, n_heads, q_col, k_col, v_col, lam_init):
    tq = min(512, seq)
    tk = min(512, seq)
    assert seq % tk == 0 and tk % tq == 0 and (tq & (tq - 1)) == 0
    nq = seq // tq
    small = lambda n: pl.BlockSpec((1, n), lambda b, h, i: (0, 0))
    kern = functools.partial(_diff_attn_kernel, tq=tq, tk=tk, lam_init=lam_init)
    return pl.pallas_call(
        kern,
        out_shape=jax.ShapeDtypeStruct((batch * seq, n_heads * HEAD_DIM), BF16),
        grid=(batch, n_heads, nq),
        in_specs=[small(DIFF_SUB)] * 4 + [
            pl.BlockSpec((HEAD_DIM, 1), lambda b, h, i: (0, 0)),
            pl.BlockSpec((tq, HEAD_DIM), lambda b, h, i: (b * nq + i, q_col + h)),
            pl.BlockSpec((seq, HEAD_DIM), lambda b, h, i: (b, k_col + h)),
            pl.BlockSpec((seq, HEAD_DIM), lambda b, h, i: (b, v_col + h)),
        ],
        out_specs=pl.BlockSpec((tq, HEAD_DIM), lambda b, h, i: (b * nq + i, h)),
        scratch_shapes=[
            pltpu.VMEM((2 * tq, HEAD_DIM), BF16),
            pltpu.VMEM((seq // tk, VT_ROWS, tk), BF16),
            pltpu.VMEM((2, tk, 2 * tq), F32),
            pltpu.VMEM((1, 2 * tq), F32),
            pltpu.VMEM((VT_ROWS, 2 * tq), F32),
        ],
        compiler_params=_cparams(3, 32),
        name="diff_attn",
    )(*lam_params, subln_g, proj, proj, proj)


def _moba_attn_kernel(q_ref, k_ref, v_ref, o_ref, kmh_ref, kml_ref, vt_ref, sel_ref, s_ref, m_ref, acc_ref,
                      *, n_blocks, t):
    i = pl.program_id(2)
    blk = MOBA_BLOCK
    per = t // blk
    nbp = kmh_ref.shape[0]

    @pl.when(i == 0)
    def _():
        _store_v_transposed(v_ref, vt_ref, t)
        kmean = jnp.sum(k_ref[...].astype(F32).reshape(n_blocks, blk, HEAD_DIM), axis=1) * (1.0 / blk)
        if n_blocks < nbp:
            kmean = jnp.concatenate([kmean, jnp.zeros((nbp - n_blocks, HEAD_DIM), F32)], axis=0)
        hi = kmean.astype(BF16)
        kmh_ref[...] = hi
        kml_ref[...] = (kmean - hi.astype(F32)).astype(BF16)

    q = q_ref[...]
    gate = (lax.dot_general(kmh_ref[...], q, _NT, preferred_element_type=F32)
            + lax.dot_general(kml_ref[...], q, _NT, preferred_element_type=F32))
    blk_id = lax.broadcasted_iota(jnp.int32, (nbp, t), 0)
    q_blk = i * per + lax.broadcasted_iota(jnp.int32, (nbp, t), 1) // blk
    blk_f = blk_id.astype(F32)
    gate = jnp.where(blk_id < q_blk, gate, NEG_INF)
    sel = jnp.zeros((nbp, t), F32)
    for _ in range(min(MOBA_TOPK, n_blocks)):
        mx = jnp.max(gate, axis=0, keepdims=True)
        cand = jnp.where(jnp.logical_and(gate == mx, gate > NEG_INF), blk_f, float(nbp))
        pick = blk_f == jnp.min(cand, axis=0, keepdims=True)
        sel = jnp.where(pick, 1.0, sel)
        gate = jnp.where(pick, NEG_INF, gate)
    sel_ref[...] = sel

    def scores(g):
        k = k_ref[pl.ds(pl.multiple_of(g * t, t), t), :]
        return lax.dot_general(k, q, _NT, preferred_element_type=F32)

    def picked(g):
        rows = [jnp.broadcast_to(sel_ref[pl.ds(g * per + c, 1), :] > 0.5, (blk, t)) for c in range(per)]
        return rows[0] if per == 1 else jnp.concatenate(rows, axis=0)

    def past_scores(g):
        return jnp.where(picked(g), scores(g), NEG_INF)

    def diag_scores():
        r = lax.broadcasted_iota(jnp.int32, (t, t), 0)
        c = lax.broadcasted_iota(jnp.int32, (t, t), 1)
        own_causal = jnp.logical_and(r // blk == c // blk, r <= c)
        return jnp.where(jnp.logical_or(own_causal, picked(i)), scores(i), NEG_INF)

    _flash_pipeline(i, i, diag_scores, past_scores, vt_ref, s_ref, m_ref, acc_ref)
    o_ref[...] = _normalized(acc_ref).T.astype(o_ref.dtype)


def _moba_attn(proj, batch, seq, n_heads, q_col, k_col, v_col):
    t = min(2 * MOBA_BLOCK, seq)
    assert seq % t == 0 and t % MOBA_BLOCK == 0
    nb = seq // MOBA_BLOCK
    nbp = -(-nb // BF16_SUBLANES) * BF16_SUBLANES
    nt = seq // t
    kern = functools.partial(_moba_attn_kernel, n_blocks=nb, t=t)
    return pl.pallas_call(
        kern,
        out_shape=jax.ShapeDtypeStruct((batch * seq, n_heads * HEAD_DIM), BF16),
        grid=(batch, n_heads, nt),
        in_specs=[
            pl.BlockSpec((t, HEAD_DIM), lambda b, h, i: (b * nt + i, q_col + h)),
            pl.BlockSpec((seq, HEAD_DIM), lambda b, h, i: (b, k_col + h)),
            pl.BlockSpec((seq, HEAD_DIM), lambda b, h, i: (b, v_col + h)),
        ],
        out_specs=pl.BlockSpec((t, HEAD_DIM), lambda b, h, i: (b * nt + i, h)),
        scratch_shapes=[
            pltpu.VMEM((nbp, HEAD_DIM), BF16),
            pltpu.VMEM((nbp, HEAD_DIM), BF16),
            pltpu.VMEM((nt, VT_ROWS, t), BF16),
            pltpu.VMEM((nbp, t), F32),
            pltpu.VMEM((2, t, t), F32),
            pltpu.VMEM((1, t), F32),
            pltpu.VMEM((VT_ROWS, t), F32),
        ],
        compiler_params=_cparams(3, 32),
        name="moba_attn",
    )(proj, proj, proj)


def _layer_norm(y, g, b):
    mu = jnp.mean(y, axis=-1, keepdims=True)
    yc = y - mu
    var = jnp.mean(jnp.square(yc), axis=-1, keepdims=True)
    return yc * lax.rsqrt(var + LN_EPS) * g + b


def _out_ln1_kernel(a_ref, b_ref, wa_ref, wb_ref, x_ref, g_ref, beta_ref, h_ref, hb_ref):
    for ch in range(x_ref.shape[0] // ROW_CHUNK):
        rows = slice(ch * ROW_CHUNK, (ch + 1) * ROW_CHUNK)
        mix = (jnp.dot(a_ref[rows, :], wa_ref[...], preferred_element_type=F32)
               + jnp.dot(b_ref[rows, :], wb_ref[...], preferred_element_type=F32))
        h = _layer_norm(DEEPNORM_ALPHA * x_ref[rows, :] + mix, g_ref[...], beta_ref[...])
        h_ref[rows, :] = h
        hb_ref[rows, :] = h.astype(hb_ref.dtype)


def _out_ln1(a_out, b_out, w_out, x2, g, beta):
    m, d = x2.shape
    wa, wb = a_out.shape[1], b_out.shape[1]
    bm = 512
    assert m % bm == 0
    row = lambda w: pl.BlockSpec((bm, w), lambda i: (i, 0))
    vec = pl.BlockSpec((1, d), lambda i: (0, 0))
    return pl.pallas_call(
        _out_ln1_kernel,
        out_shape=(jax.ShapeDtypeStruct((m, d), F32), jax.ShapeDtypeStruct((m, d), BF16)),
        grid=(m // bm,),
        in_specs=[
            row(wa), row(wb),
            pl.BlockSpec((wa, d), lambda i: (0, 0)),
            pl.BlockSpec((wb, d), lambda i: (wa // wb, 0)),
            row(d), vec, vec,
        ],
        out_specs=(row(d), row(d)),
        compiler_params=_cparams(1, 56),
        name="out_ln1",
    )(a_out, b_out, w_out, w_out, x2, g, beta)


def _ffn_up_kernel(h_ref, wg32_ref, wv32_ref, cw_ref, cb_ref, o_ref, wg_ref, wv_ref, gbuf_ref,
                   *, bm, tiles_per_seq):
    r = pl.program_id(1)

    @pl.when(r == 0)
    def _():
        wg_ref[...] = wg32_ref[...].astype(BF16)
        wv_ref[...] = wv32_ref[...].astype(BF16)

    @pl.when(r % tiles_per_seq == 0)
    def _():
        gbuf_ref[0:CONV_HALO, :] = jnp.zeros((CONV_HALO, gbuf_ref.shape[1]), F32)

    cw = cw_ref[...]
    for c in range(bm // ROW_CHUNK):
        lo = c * ROW_CHUNK
        h = h_ref[lo:lo + ROW_CHUNK, :]
        g = jnp.dot(h, wg_ref[...], preferred_element_type=F32)
        val = jnp.dot(h, wv_ref[...], preferred_element_type=F32)
        gbuf_ref[CONV_HALO + lo:CONV_HALO + lo + ROW_CHUNK, :] = g
        g1 = gbuf_ref[pl.ds(CONV_HALO + lo - 1, ROW_CHUNK), :]
        g2 = gbuf_ref[pl.ds(CONV_HALO + lo - 2, ROW_CHUNK), :]
        gc = cb_ref[...] + (cw[0:1, :] * g2 + cw[1:2, :] * g1 + cw[2:3, :] * g)
        act = gc * jax.nn.sigmoid(gc) * val
        o_ref[lo:lo + ROW_CHUNK, :] = act.astype(o_ref.dtype)
    gbuf_ref[0:CONV_HALO, :] = gbuf_ref[bm:bm + CONV_HALO, :]


def _ffn_up(hb, w_up, conv_w, conv_b, seq):
    m, d = hb.shape
    f = conv_w.shape[1]
    bm = min(1024, seq)
    bn = 512
    assert seq % bm == 0 and f % bn == 0
    nct = f // bn
    kern = functools.partial(_ffn_up_kernel, bm=bm, tiles_per_seq=seq // bm)
    return pl.pallas_call(
        kern,
        out_shape=jax.ShapeDtypeStruct((m, f), BF16),
        grid=(nct, m // bm),
        in_specs=[
            pl.BlockSpec((bm, d), lambda c, r: (r, 0)),
            pl.BlockSpec((d, bn), lambda c, r: (0, c)),
            pl.BlockSpec((d, bn), lambda c, r: (0, nct + c)),
            pl.BlockSpec((CONV_W, bn), lambda c, r: (0, c)),
            pl.BlockSpec((1, bn), lambda c, r: (0, c)),
        ],
        out_specs=pl.BlockSpec((bm, bn), lambda c, r: (r, c)),
        scratch_shapes=[pltpu.VMEM((d, bn), BF16), pltpu.VMEM((d, bn), BF16),
                        pltpu.VMEM((bm + CONV_HALO, bn), F32)],
        compiler_params=_cparams(2, 48),
        name="ffn_up",
    )(hb, w_up, w_up, conv_w, conv_b)


def _ffn_down_kernel(a_ref, w_ref, h_ref, g_ref, beta_ref, o_ref):
    k = pl.program_id(1)
    last = pl.num_programs(1) - 1

    def chunks(step):
        for ch in range(o_ref.shape[0] // ROW_CHUNK):
            rows = slice(ch * ROW_CHUNK, (ch + 1) * ROW_CHUNK)
            step(rows, jnp.dot(a_ref[rows, :], w_ref[...], preferred_element_type=F32))

    def first(rows, part):
        o_ref[rows, :] = DEEPNORM_ALPHA * h_ref[rows, :] + part

    def middle(rows, part):
        o_ref[rows, :] += part

    def final(rows, part):
        o_ref[rows, :] = _layer_norm(o_ref[rows, :] + part, g_ref[...], beta_ref[...])

    pl.when(k == 0)(lambda: chunks(first))
    pl.when(jnp.logical_and(k > 0, k < last))(lambda: chunks(middle))
    pl.when(k == last)(lambda: chunks(final))


def _ffn_down(act, w_down, h1, g, beta):
    m, f = act.shape
    d = w_down.shape[1]
    bm, bk = 1024, 512
    assert m % bm == 0 and f % bk == 0 and f // bk >= 2
    vec = pl.BlockSpec((1, d), lambda i, k: (0, 0))
    return pl.pallas_call(
        _ffn_down_kernel,
        out_shape=jax.ShapeDtypeStruct((m, d), F32),
        grid=(m // bm, f // bk),
        in_specs=[
            pl.BlockSpec((bm, bk), lambda i, k: (i, k)),
            pl.BlockSpec((bk, d), lambda i, k: (k, 0)),
            pl.BlockSpec((bm, d), lambda i, k: (i, 0)),
            vec, vec,
        ],
        out_specs=pl.BlockSpec((bm, d), lambda i, k: (i, 0)),
        compiler_params=_cparams(2, 48),
        name="ffn_down",
    )(act, w_down, h1, g, beta)


def kernel(x, w_in, lambda_q1, lambda_k1, lambda_q2, lambda_k2, subln_g, w_out, ln1_g, ln1_b, w_up,
           conv_w, conv_b, w_down, ln2_g, ln2_b):
    batch, seq, d_model = x.shape
    depth = w_in.shape[0]
    mix_w = w_out.shape[1]
    diff_w = moba_w = mix_w // 2
    n_dh, n_mh = diff_w // HEAD_DIM, moba_w // HEAD_DIM
    assert w_in.shape[2] == 3 * diff_w + 3 * moba_w

    h = x.reshape(batch * seq, d_model)
    for l in range(depth):
        lam_init = 0.8 - 0.6 * math.exp(-0.3 * l)
        proj = _proj_rope(h, w_in[l].astype(BF16), seq, diff_w, moba_w)
        lam_params = [p[l].reshape(1, DIFF_SUB) for p in (lambda_q1, lambda_k1, lambda_q2, lambda_k2)]
        a_out = _diff_attn(proj, lam_params, subln_g[l].reshape(HEAD_DIM, 1), batch, seq, n_dh,
                           0, n_dh, 2 * n_dh, lam_init)
        b_out = _moba_attn(proj, batch, seq, n_mh, 3 * n_dh, 3 * n_dh + n_mh, 3 * n_dh + 2 * n_mh)
        h1, h1b = _out_ln1(a_out, b_out, w_out[l].astype(BF16), h,
                           ln1_g[l].reshape(1, d_model), ln1_b[l].reshape(1, d_model))
        act = _ffn_up(h1b, w_up[l], conv_w[l], conv_b[l].reshape(1, -1), seq)
        h = _ffn_down(act, w_down[l].astype(BF16), h1, ln2_g[l].reshape(1, d_model),
                      ln2_b[l].reshape(1, d_model))
    return h.reshape(batch, seq, d_model)
```

```python
import functools
import math

import jax
import jax.numpy as jnp
from jax import lax
from jax.experimental import pallas as pl
from jax.experimental.pallas import tpu as pltpu

F32 = jnp.float32
BF16 = jnp.bfloat16

LANES = 128
BF16_SUBLANES = 16
HEAD_DIM = 128
VT_ROWS = HEAD_DIM + BF16_SUBLANES
LOG2E = 1.4426950408889634
DIFF_SUB = HEAD_DIM // 2
ROPE_THETA = 500000.0
ROT_FRACTION = 4
MOBA_BLOCK = 256
MOBA_TOPK = 3
CONV_W = 3
CONV_HALO = 8
ROW_CHUNK = 256
LN_EPS = 1e-5
RMS_EPS = 1e-5
DEPTH = 1
DEEPNORM_ALPHA = (2.0 * DEPTH) ** 0.25
DIFF_SCALE = DIFF_SUB ** -0.5
MOBA_SCALE = HEAD_DIM ** -0.5
NEG_INF = float("-inf")

_NT = (((1,), (1,)), ((), ()))


def _cparams(n_axes, vmem_mib):
    return pltpu.CompilerParams(
        dimension_semantics=("arbitrary",) * n_axes,
        vmem_limit_bytes=vmem_mib * 1024 * 1024,
    )


def _rope_coeffs(seq, period):
    rot = period // ROT_FRACTION
    half = rot // 2
    inv = 1.0 / (ROPE_THETA ** (jnp.arange(0, rot, 2, dtype=F32) / rot))
    dim = [l % period for l in range(LANES)]
    lane_inv = inv[jnp.asarray([d % half for d in dim])]
    first = jnp.asarray([d < half for d in dim])[None, :]
    second = jnp.asarray([half <= d < rot for d in dim])[None, :]
    ang = jnp.arange(seq, dtype=F32)[:, None] * lane_inv[None, :]
    cos, sin = jnp.cos(ang), jnp.sin(ang)
    c = jnp.where(jnp.logical_or(first, second), cos, 1.0)
    s1 = jnp.where(first, -sin, 0.0)
    s2 = jnp.where(second, sin, 0.0)
    return jnp.stack([c, s1, s2], axis=0)


def _proj_rope_kernel(x_ref, w_ref, tab_ref, o_ref, xb_ref, *, diff_tiles, moba_tiles, q_scales,
                      n_heads_blk):
    j = pl.program_id(1)
    scale = jnp.float32(1.0)
    for t, s in q_scales:
        scale = jnp.where(j == t, jnp.float32(s), scale)

    def tile(half, first):
        for ch in range(x_ref.shape[0] // ROW_CHUNK):
            rows = slice(ch * ROW_CHUNK, (ch + 1) * ROW_CHUNK)
            if first:
                xb_ref[rows, :] = x_ref[rows, :].astype(BF16)
            acc = jnp.dot(xb_ref[rows, :], w_ref[...], preferred_element_type=F32)
            if half is None:
                o_ref[rows, :] = acc.astype(o_ref.dtype)
                continue
            c, s1, s2 = tab_ref[0, 0, rows, :], tab_ref[0, 1, rows, :], tab_ref[0, 2, rows, :]
            for hb in range(n_heads_blk):
                a = acc[:, hb * LANES:(hb + 1) * LANES]
                o = a * c + pltpu.roll(a, LANES - half, 1) * s1 + pltpu.roll(a, half, 1) * s2
                o_ref[rows, hb * LANES:(hb + 1) * LANES] = (o * scale).astype(o_ref.dtype)

    assert diff_tiles[0] == 0
    is_diff = functools.reduce(jnp.logical_or, [j == t for t in diff_tiles[1:]])
    is_moba = functools.reduce(jnp.logical_or, [j == t for t in moba_tiles])
    diff_half = DIFF_SUB // ROT_FRACTION // 2
    moba_half = HEAD_DIM // ROT_FRACTION // 2

    @pl.when(j == 0)
    def _():
        tile(diff_half, True)

    @pl.when(is_diff)
    def _():
        tile(diff_half, False)

    @pl.when(is_moba)
    def _():
        tile(moba_half, False)

    @pl.when(jnp.logical_not(jnp.logical_or(j == 0, jnp.logical_or(is_diff, is_moba))))
    def _():
        tile(None, False)


def _proj_rope(x2, w_in, seq, diff_w, moba_w):
    m, d = x2.shape
    n = w_in.shape[1]
    bm = min(1024, seq)
    bn = 1024
    assert seq % bm == 0 and m % bm == 0 and diff_w % bn == 0 and moba_w % bn == 0
    dt, mt = diff_w // bn, moba_w // bn
    q_d_tiles = list(range(0, dt))
    k_d_tiles = list(range(dt, 2 * dt))
    q_m_tiles = list(range(3 * dt, 3 * dt + mt))
    k_m_tiles = list(range(3 * dt + mt, 3 * dt + 2 * mt))
    tabs = jnp.stack([_rope_coeffs(seq, DIFF_SUB), _rope_coeffs(seq, HEAD_DIM)], axis=0)

    def kind(j):
        k = jnp.int32(0)
        for t in q_m_tiles + k_m_tiles:
            k = jnp.where(j == t, 1, k)
        return k

    spt = seq // bm
    q_scales = ([(t, DIFF_SCALE * LOG2E) for t in q_d_tiles] + [(t, MOBA_SCALE * LOG2E) for t in q_m_tiles])
    kern = functools.partial(_proj_rope_kernel, diff_tiles=q_d_tiles + k_d_tiles,
                             moba_tiles=q_m_tiles + k_m_tiles, q_scales=q_scales, n_heads_blk=bn // LANES)
    return pl.pallas_call(
        kern,
        out_shape=jax.ShapeDtypeStruct((m, n), BF16),
        grid=(m // bm, n // bn),
        in_specs=[
            pl.BlockSpec((bm, d), lambda i, j: (i, 0)),
            pl.BlockSpec((d, bn), lambda i, j: (0, j)),
            pl.BlockSpec((1, 3, bm, LANES), lambda i, j: (kind(j), 0, i % spt, 0)),
        ],
        out_specs=pl.BlockSpec((bm, bn), lambda i, j: (i, j)),
        scratch_shapes=[pltpu.VMEM((bm, d), BF16)],
        compiler_params=_cparams(2, 56),
        name="proj_rope",
    )(x2, w_in, tabs)


def _softmax_step(s_t, v_t, m_ref, acc_ref):
    m_prev = m_ref[...]
    m_new = jnp.maximum(m_prev, jnp.max(s_t, axis=0, keepdims=True))
    alpha = jnp.exp2(m_prev - m_new)
    p = jnp.exp2(s_t - m_new)
    acc_ref[...] = acc_ref[...] * alpha + jnp.dot(v_t, p.astype(v_t.dtype), preferred_element_type=F32)
    m_ref[...] = m_new


def _flash_pipeline(n_full, j_diag, diag_scores, full_scores, vt_ref, s_ref, slots, m_ref, acc_ref):
    sa, sb = slots
    m_ref[...] = jnp.full(m_ref.shape, NEG_INF, F32)
    acc_ref[...] = jnp.zeros(acc_ref.shape, F32)
    s_ref[sa] = diag_scores()
    n_pairs = (n_full + 1) // 2

    def pair(p, carry):
        s_ref[sb] = full_scores(2 * p)
        _softmax_step(s_ref[sa], vt_ref[jnp.where(p == 0, j_diag, 2 * p - 1)], m_ref, acc_ref)
        s_ref[sa] = full_scores(jnp.minimum(2 * p + 1, max(n_full - 1, 0)))
        _softmax_step(s_ref[sb], vt_ref[2 * p], m_ref, acc_ref)
        return carry

    if n_pairs > 0:
        lax.fori_loop(0, n_pairs, pair, 0)
    if (n_full + 1) % 2 == 1:
        _softmax_step(s_ref[sa], vt_ref[j_diag if n_full == 0 else n_full - 1], m_ref, acc_ref)


def _store_v_transposed(v_ref, vt_ref, tk):
    pad = VT_ROWS - HEAD_DIM
    ones_row = (lax.broadcasted_iota(jnp.int32, (pad, tk), 0) == 0).astype(vt_ref.dtype)
    for j in range(vt_ref.shape[0]):
        vt_ref[j, 0:HEAD_DIM, :] = v_ref[j * tk:(j + 1) * tk, :].T
        vt_ref[j, HEAD_DIM:VT_ROWS, :] = ones_row


def _normalized(acc_ref):
    return acc_ref[0:HEAD_DIM, :] / acc_ref[HEAD_DIM:HEAD_DIM + 1, :]


def _diff_attn_kernel(lq1_ref, lk1_ref, lq2_ref, lk2_ref, g_ref, q_ref, k_ref, v_ref, o_ref,
                      qs_ref, vt_ref, s_ref, m_ref, acc_ref, *, tq, tk, lam_init):
    seq = q_ref.shape[0]
    _store_v_transposed(v_ref, vt_ref, tk)
    lam = (jnp.exp(jnp.sum(lq1_ref[...] * lk1_ref[...])) - jnp.exp(jnp.sum(lq2_ref[...] * lk2_ref[...]))
           + lam_init)

    for i in range(seq // tq):
        par = i % 2
        qs, m_par, acc_par = qs_ref.at[par], m_ref.at[par], acc_ref.at[par]
        q = q_ref[i * tq:(i + 1) * tq, :]
        lane = lax.broadcasted_iota(jnp.int32, q.shape, 1)
        zero = jnp.zeros_like(q)
        qs[0:tq, :] = jnp.where(lane < DIFF_SUB, q, zero)
        qs[tq:2 * tq, :] = jnp.where(lane >= DIFF_SUB, q, zero)

        def scores(j, qs=qs):
            k = k_ref[pl.ds(pl.multiple_of(j * tk, tk), tk), :]
            return lax.dot_general(k, qs[...], _NT, preferred_element_type=F32)

        j_diag = (i * tq) // tk

        def diag_scores(i=i, j_diag=j_diag, scores=scores):
            s_t = scores(j_diag)
            kpos = j_diag * tk + lax.broadcasted_iota(jnp.int32, s_t.shape, 0)
            qpos = i * tq + (lax.broadcasted_iota(jnp.int32, s_t.shape, 1) & (tq - 1))
            return jnp.where(kpos <= qpos, s_t, NEG_INF)

        _flash_pipeline(j_diag, j_diag, diag_scores, scores, vt_ref, s_ref, (2 * par, 2 * par + 1),
                        m_par, acc_par)

        o_t = _normalized(acc_par)
        o_t = o_t[:, 0:tq] - lam * o_t[:, tq:2 * tq]
        o_t = o_t * lax.rsqrt(jnp.mean(jnp.square(o_t), axis=0, keepdims=True) + RMS_EPS)
        o_t = o_t * g_ref[...] * (1.0 - lam_init)
        o_ref[i * tq:(i + 1) * tq, :] = o_t.T.astype(o_ref.dtype)


def _diff_attn(proj, lam_params, subln_g, batch, seq, n_heads, q_col, k_col, v_col, lam_init):
    tq = min(512, seq)
    tk = min(512, seq)
    assert seq % tk == 0 and tk % tq == 0 and (tq & (tq - 1)) == 0
    small = lambda n: pl.BlockSpec((1, n), lambda b, h: (0, 0))
    head = lambda col: pl.BlockSpec((seq, HEAD_DIM), lambda b, h: (b, col + h))
    kern = functools.partial(_diff_attn_kernel, tq=tq, tk=tk, lam_init=lam_init)
    return pl.pallas_call(
        kern,
        out_shape=jax.ShapeDtypeStruct((batch * seq, n_heads * HEAD_DIM), BF16),
        grid=(batch, n_heads),
        in_specs=[small(DIFF_SUB)] * 4 + [
            pl.BlockSpec((HEAD_DIM, 1), lambda b, h: (0, 0)),
            head(q_col), head(k_col), head(v_col),
        ],
        out_specs=head(0),
        scratch_shapes=[
            pltpu.VMEM((2, 2 * tq, HEAD_DIM), BF16),
            pltpu.VMEM((seq // tk, VT_ROWS, tk), BF16),
            pltpu.VMEM((4, tk, 2 * tq), F32),
            pltpu.VMEM((2, 1, 2 * tq), F32),
            pltpu.VMEM((2, VT_ROWS, 2 * tq), F32),
        ],
        compiler_params=_cparams(2, 40),
        name="diff_attn",
    )(*lam_params, subln_g, proj, proj, proj)


def _moba_attn_kernel(q_ref, k_ref, v_ref, o_ref, kmh_ref, kml_ref, vt_ref, sel_ref, s_ref, m_ref, acc_ref,
                      *, n_blocks, t):
    seq = q_ref.shape[0]
    blk = MOBA_BLOCK
    per = t // blk
    nbp = kmh_ref.shape[0]

    _store_v_transposed(v_ref, vt_ref, t)
    kmean = jnp.sum(k_ref[...].astype(F32).reshape(n_blocks, blk, HEAD_DIM), axis=1) * (1.0 / blk)
    if n_blocks < nbp:
        kmean = jnp.concatenate([kmean, jnp.zeros((nbp - n_blocks, HEAD_DIM), F32)], axis=0)
    hi = kmean.astype(BF16)
    kmh_ref[...] = hi
    kml_ref[...] = (kmean - hi.astype(F32)).astype(BF16)

    q_all = q_ref[...]
    gate = (lax.dot_general(kmh_ref[...], q_all, _NT, preferred_element_type=F32)
            + lax.dot_general(kml_ref[...], q_all, _NT, preferred_element_type=F32))
    blk_id = lax.broadcasted_iota(jnp.int32, (nbp, seq), 0)
    q_blk = lax.broadcasted_iota(jnp.int32, (nbp, seq), 1) // blk
    blk_f = blk_id.astype(F32)
    gate = jnp.where(blk_id < q_blk, gate, NEG_INF)
    sel = jnp.zeros((nbp, seq), F32)
    for _ in range(min(MOBA_TOPK, n_blocks)):
        mx = jnp.max(gate, axis=0, keepdims=True)
        cand = jnp.where(jnp.logical_and(gate == mx, gate > NEG_INF), blk_f, float(nbp))
        pick = blk_f == jnp.min(cand, axis=0, keepdims=True)
        sel = jnp.where(pick, 1.0, sel)
        gate = jnp.where(pick, NEG_INF, gate)
    sel_ref[...] = sel

    for i in range(seq // t):
        par = i % 2
        m_par, acc_par = m_ref.at[par], acc_ref.at[par]
        cols = slice(i * t, (i + 1) * t)

        def scores(g, cols=cols):
            k = k_ref[pl.ds(pl.multiple_of(g * t, t), t), :]
            return lax.dot_general(k, q_ref[cols, :], _NT, preferred_element_type=F32)

        def picked(g, cols=cols):
            rows = [jnp.broadcast_to(sel_ref[pl.ds(g * per + c, 1), cols] > 0.5, (blk, t))
                    for c in range(per)]
            return rows[0] if per == 1 else jnp.concatenate(rows, axis=0)

        def past_scores(g, scores=scores, picked=picked):
            return jnp.where(picked(g), scores(g), NEG_INF)

        def diag_scores(i=i, scores=scores, picked=picked):
            r = lax.broadcasted_iota(jnp.int32, (t, t), 0)
            c = lax.broadcasted_iota(jnp.int32, (t, t), 1)
            own_causal = jnp.logical_and(r // blk == c // blk, r <= c)
            return jnp.where(jnp.logical_or(own_causal, picked(i)), scores(i), NEG_INF)

        _flash_pipeline(i, i, diag_scores, past_scores, vt_ref, s_ref, (2 * par, 2 * par + 1),
                        m_par, acc_par)
        o_ref[cols, :] = _normalized(acc_par).T.astype(o_ref.dtype)


def _moba_attn(proj, batch, seq, n_heads, q_col, k_col, v_col):
    t = min(2 * MOBA_BLOCK, seq)
    assert seq % t == 0 and t % MOBA_BLOCK == 0
    nb = seq // MOBA_BLOCK
    nbp = -(-nb // BF16_SUBLANES) * BF16_SUBLANES
    nt = seq // t
    head = lambda col: pl.BlockSpec((seq, HEAD_DIM), lambda b, h: (b, col + h))
    kern = functools.partial(_moba_attn_kernel, n_blocks=nb, t=t)
    return pl.pallas_call(
        kern,
        out_shape=jax.ShapeDtypeStruct((batch * seq, n_heads * HEAD_DIM), BF16),
        grid=(batch, n_heads),
        in_specs=[head(q_col), head(k_col), head(v_col)],
        out_specs=head(0),
        scratch_shapes=[
            pltpu.VMEM((nbp, HEAD_DIM), BF16),
            pltpu.VMEM((nbp, HEAD_DIM), BF16),
            pltpu.VMEM((nt, VT_ROWS, t), BF16),
            pltpu.VMEM((nbp, seq), F32),
            pltpu.VMEM((4, t, t), F32),
            pltpu.VMEM((2, 1, t), F32),
            pltpu.VMEM((2, VT_ROWS, t), F32),
        ],
        compiler_params=_cparams(2, 40),
        name="moba_attn",
    )(proj, proj, proj)


def _layer_norm(y, g, b):
    mu = jnp.mean(y, axis=-1, keepdims=True)
    yc = y - mu
    var = jnp.mean(jnp.square(yc), axis=-1, keepdims=True)
    return yc * lax.rsqrt(var + LN_EPS) * g + b


def _out_ln1_kernel(a_ref, b_ref, wa_ref, wb_ref, x_ref, g_ref, beta_ref, h_ref, hb_ref):
    for ch in range(x_ref.shape[0] // ROW_CHUNK):
        rows = slice(ch * ROW_CHUNK, (ch + 1) * ROW_CHUNK)
        mix = (jnp.dot(a_ref[rows, :], wa_ref[...], preferred_element_type=F32)
               + jnp.dot(b_ref[rows, :], wb_ref[...], preferred_element_type=F32))
        h = _layer_norm(DEEPNORM_ALPHA * x_ref[rows, :] + mix, g_ref[...], beta_ref[...])
        h_ref[rows, :] = h
        hb_ref[rows, :] = h.astype(hb_ref.dtype)


def _out_ln1(a_out, b_out, w_out, x2, g, beta):
    m, d = x2.shape
    wa, wb = a_out.shape[1], b_out.shape[1]
    bm = 512
    assert m % bm == 0
    row = lambda w: pl.BlockSpec((bm, w), lambda i: (i, 0))
    vec = pl.BlockSpec((1, d), lambda i: (0, 0))
    return pl.pallas_call(
        _out_ln1_kernel,
        out_shape=(jax.ShapeDtypeStruct((m, d), F32), jax.ShapeDtypeStruct((m, d), BF16)),
        grid=(m // bm,),
        in_specs=[
            row(wa), row(wb),
            pl.BlockSpec((wa, d), lambda i: (0, 0)),
            pl.BlockSpec((wb, d), lambda i: (wa // wb, 0)),
            row(d), vec, vec,
        ],
        out_specs=(row(d), row(d)),
        compiler_params=_cparams(1, 56),
        name="out_ln1",
    )(a_out, b_out, w_out, w_out, x2, g, beta)


def _ffn_up_kernel(h_ref, wg32_ref, wv32_ref, cw_ref, cb_ref, o_ref, wg_ref, wv_ref, gbuf_ref,
                   *, bm, tiles_per_seq):
    r = pl.program_id(1)

    @pl.when(r == 0)
    def _():
        wg_ref[...] = wg32_ref[...].astype(BF16)
        wv_ref[...] = wv32_ref[...].astype(BF16)

    @pl.when(r % tiles_per_seq == 0)
    def _():
        gbuf_ref[0:CONV_HALO, :] = jnp.zeros((CONV_HALO, gbuf_ref.shape[1]), F32)

    cw = cw_ref[...]
    for c in range(bm // ROW_CHUNK):
        lo = c * ROW_CHUNK
        h = h_ref[lo:lo + ROW_CHUNK, :]
        g = jnp.dot(h, wg_ref[...], preferred_element_type=F32)
        val = jnp.dot(h, wv_ref[...], preferred_element_type=F32)
        gbuf_ref[CONV_HALO + lo:CONV_HALO + lo + ROW_CHUNK, :] = g
        g1 = gbuf_ref[pl.ds(CONV_HALO + lo - 1, ROW_CHUNK), :]
        g2 = gbuf_ref[pl.ds(CONV_HALO + lo - 2, ROW_CHUNK), :]
        gc = cb_ref[...] + (cw[0:1, :] * g2 + cw[1:2, :] * g1 + cw[2:3, :] * g)
        act = gc * jax.nn.sigmoid(gc) * val
        o_ref[lo:lo + ROW_CHUNK, :] = act.astype(o_ref.dtype)
    gbuf_ref[0:CONV_HALO, :] = gbuf_ref[bm:bm + CONV_HALO, :]


def _ffn_up(hb, w_up, conv_w, conv_b, seq):
    m, d = hb.shape
    f = conv_w.shape[1]
    bm = min(2048, seq)
    bn = 512
    assert seq % bm == 0 and f % bn == 0
    nct = f // bn
    kern = functools.partial(_ffn_up_kernel, bm=bm, tiles_per_seq=seq // bm)
    return pl.pallas_call(
        kern,
        out_shape=jax.ShapeDtypeStruct((m, f), BF16),
        grid=(nct, m // bm),
        in_specs=[
            pl.BlockSpec((bm, d), lambda c, r: (r, 0)),
            pl.BlockSpec((d, bn), lambda c, r: (0, c)),
            pl.BlockSpec((d, bn), lambda c, r: (0, nct + c)),
            pl.BlockSpec((CONV_W, bn), lambda c, r: (0, c)),
            pl.BlockSpec((1, bn), lambda c, r: (0, c)),
        ],
        out_specs=pl.BlockSpec((bm, bn), lambda c, r: (r, c)),
        scratch_shapes=[pltpu.VMEM((d, bn), BF16), pltpu.VMEM((d, bn), BF16),
                        pltpu.VMEM((bm + CONV_HALO, bn), F32)],
        compiler_params=_cparams(2, 54),
        name="ffn_up",
    )(hb, w_up, w_up, conv_w, conv_b)


def _ffn_down_kernel(a_ref, w_ref, h_ref, g_ref, beta_ref, o_ref):
    k = pl.program_id(1)
    last = pl.num_programs(1) - 1

    def chunks(step):
        for ch in range(o_ref.shape[0] // ROW_CHUNK):
            rows = slice(ch * ROW_CHUNK, (ch + 1) * ROW_CHUNK)
            step(rows, jnp.dot(a_ref[rows, :], w_ref[...], preferred_element_type=F32))

    def first(rows, part):
        o_ref[rows, :] = DEEPNORM_ALPHA * h_ref[rows, :] + part

    def middle(rows, part):
        o_ref[rows, :] += part

    def final(rows, part):
        o_ref[rows, :] = _layer_norm(o_ref[rows, :] + part, g_ref[...], beta_ref[...])

    pl.when(k == 0)(lambda: chunks(first))
    pl.when(jnp.logical_and(k > 0, k < last))(lambda: chunks(middle))
    pl.when(k == last)(lambda: chunks(final))


def _ffn_down(act, w_down, h1, g, beta):
    m, f = act.shape
    d = w_down.shape[1]
    bm, bk = 1024, 512
    assert m % bm == 0 and f % bk == 0 and f // bk >= 2
    vec = pl.BlockSpec((1, d), lambda i, k: (0, 0))
    return pl.pallas_call(
        _ffn_down_kernel,
        out_shape=jax.ShapeDtypeStruct((m, d), F32),
        grid=(m // bm, f // bk),
        in_specs=[
            pl.BlockSpec((bm, bk), lambda i, k: (i, k)),
            pl.BlockSpec((bk, d), lambda i, k: (k, 0)),
            pl.BlockSpec((bm, d), lambda i, k: (i, 0)),
            vec, vec,
        ],
        out_specs=pl.BlockSpec((bm, d), lambda i, k: (i, 0)),
        compiler_params=_cparams(2, 48),
        name="ffn_down",
    )(act, w_down, h1, g, beta)


def kernel(x, w_in, lambda_q1, lambda_k1, lambda_q2, lambda_k2, subln_g, w_out, ln1_g, ln1_b, w_up,
           conv_w, conv_b, w_down, ln2_g, ln2_b):
    batch, seq, d_model = x.shape
    depth = w_in.shape[0]
    mix_w = w_out.shape[1]
    diff_w = moba_w = mix_w // 2
    n_dh, n_mh = diff_w // HEAD_DIM, moba_w // HEAD_DIM
    assert w_in.shape[2] == 3 * diff_w + 3 * moba_w

    h = x.reshape(batch * seq, d_model)
    for l in range(depth):
        lam_init = 0.8 - 0.6 * math.exp(-0.3 * l)
        proj = _proj_rope(h, w_in[l].astype(BF16), seq, diff_w, moba_w)
        lam_params = [p[l].reshape(1, DIFF_SUB) for p in (lambda_q1, lambda_k1, lambda_q2, lambda_k2)]
        a_out = _diff_attn(proj, lam_params, subln_g[l].reshape(HEAD_DIM, 1), batch, seq, n_dh,
                           0, n_dh, 2 * n_dh, lam_init)
        b_out = _moba_attn(proj, batch, seq, n_mh, 3 * n_dh, 3 * n_dh + n_mh, 3 * n_dh + 2 * n_mh)
        h1, h1b = _out_ln1(a_out, b_out, w_out[l].astype(BF16), h,
                           ln1_g[l].reshape(1, d_model), ln1_b[l].reshape(1, d_model))
        act = _ffn_up(h1b, w_up[l], conv_w[l], conv_b[l].reshape(1, -1), seq)
        h = _ffn_down(act, w_down[l].astype(BF16), h1, ln2_g[l].reshape(1, d_model),
                      ln2_b[l].reshape(1, d_model))
    return h.reshape(batch, seq, d_model)
```

```python
import functools
import math

import jax
import jax.numpy as jnp
from jax import lax
from jax.experimental import pallas as pl
from jax.experimental.pallas import tpu as pltpu

F32 = jnp.float32
BF16 = jnp.bfloat16

LANES = 128
BF16_SUBLANES = 16
HEAD_DIM = 128
VT_ROWS = HEAD_DIM + BF16_SUBLANES
LOG2E = 1.4426950408889634
DIFF_SUB = HEAD_DIM // 2
ROPE_THETA = 500000.0
ROT_FRACTION = 4
MOBA_BLOCK = 256
MOBA_TOPK = 3
CONV_W = 3
CONV_HALO = 8
ROW_CHUNK = 256
LN_EPS = 1e-5
RMS_EPS = 1e-5
DEPTH = 1
DEEPNORM_ALPHA = (2.0 * DEPTH) ** 0.25
DIFF_SCALE = DIFF_SUB ** -0.5
MOBA_SCALE = HEAD_DIM ** -0.5
NEG_INF = float("-inf")

_NT = (((1,), (1,)), ((), ()))


def _cparams(n_axes, vmem_mib):
    return pltpu.CompilerParams(
        dimension_semantics=("arbitrary",) * n_axes,
        vmem_limit_bytes=vmem_mib * 1024 * 1024,
    )


def _rope_coeffs(seq, period):
    rot = period // ROT_FRACTION
    half = rot // 2
    inv = 1.0 / (ROPE_THETA ** (jnp.arange(0, rot, 2, dtype=F32) / rot))
    dim = [l % period for l in range(LANES)]
    lane_inv = inv[jnp.asarray([d % half for d in dim])]
    first = jnp.asarray([d < half for d in dim])[None, :]
    second = jnp.asarray([half <= d < rot for d in dim])[None, :]
    ang = jnp.arange(seq, dtype=F32)[:, None] * lane_inv[None, :]
    cos, sin = jnp.cos(ang), jnp.sin(ang)
    c = jnp.where(jnp.logical_or(first, second), cos, 1.0)
    s1 = jnp.where(first, -sin, 0.0)
    s2 = jnp.where(second, sin, 0.0)
    return jnp.stack([c, s1, s2], axis=0)


def _proj_rope_kernel(x_ref, w_ref, tab_ref, o_ref, xb_ref, *, diff_tiles, moba_tiles, q_scales,
                      n_heads_blk):
    j = pl.program_id(1)
    scale = jnp.float32(1.0)
    for t, s in q_scales:
        scale = jnp.where(j == t, jnp.float32(s), scale)

    def tile(half, first):
        for ch in range(x_ref.shape[0] // ROW_CHUNK):
            rows = slice(ch * ROW_CHUNK, (ch + 1) * ROW_CHUNK)
            if first:
                xb_ref[rows, :] = x_ref[rows, :].astype(BF16)
            acc = jnp.dot(xb_ref[rows, :], w_ref[...], preferred_element_type=F32)
            if half is None:
                o_ref[rows, :] = acc.astype(o_ref.dtype)
                continue
            c, s1, s2 = tab_ref[0, 0, rows, :], tab_ref[0, 1, rows, :], tab_ref[0, 2, rows, :]
            for hb in range(n_heads_blk):
                a = acc[:, hb * LANES:(hb + 1) * LANES]
                o = a * c + pltpu.roll(a, LANES - half, 1) * s1 + pltpu.roll(a, half, 1) * s2
                o_ref[rows, hb * LANES:(hb + 1) * LANES] = (o * scale).astype(o_ref.dtype)

    assert diff_tiles[0] == 0
    is_diff = functools.reduce(jnp.logical_or, [j == t for t in diff_tiles[1:]])
    is_moba = functools.reduce(jnp.logical_or, [j == t for t in moba_tiles])
    diff_half = DIFF_SUB // ROT_FRACTION // 2
    moba_half = HEAD_DIM // ROT_FRACTION // 2

    @pl.when(j == 0)
    def _():
        tile(diff_half, True)

    @pl.when(is_diff)
    def _():
        tile(diff_half, False)

    @pl.when(is_moba)
    def _():
        tile(moba_half, False)

    @pl.when(jnp.logical_not(jnp.logical_or(j == 0, jnp.logical_or(is_diff, is_moba))))
    def _():
        tile(None, False)


def _proj_rope(x2, w_in, seq, diff_w, moba_w):
    m, d = x2.shape
    n = w_in.shape[1]
    bm = min(1024, seq)
    bn = 1024
    assert seq % bm == 0 and m % bm == 0 and diff_w % bn == 0 and moba_w % bn == 0
    dt, mt = diff_w // bn, moba_w // bn
    q_d_tiles = list(range(0, dt))
    k_d_tiles = list(range(dt, 2 * dt))
    q_m_tiles = list(range(3 * dt, 3 * dt + mt))
    k_m_tiles = list(range(3 * dt + mt, 3 * dt + 2 * mt))
    tabs = jnp.stack([_rope_coeffs(seq, DIFF_SUB), _rope_coeffs(seq, HEAD_DIM)], axis=0)

    def kind(j):
        k = jnp.int32(0)
        for t in q_m_tiles + k_m_tiles:
            k = jnp.where(j == t, 1, k)
        return k

    spt = seq // bm
    q_scales = ([(t, DIFF_SCALE * LOG2E) for t in q_d_tiles] + [(t, MOBA_SCALE * LOG2E) for t in q_m_tiles])
    kern = functools.partial(_proj_rope_kernel, diff_tiles=q_d_tiles + k_d_tiles,
                             moba_tiles=q_m_tiles + k_m_tiles, q_scales=q_scales, n_heads_blk=bn // LANES)
    return pl.pallas_call(
        kern,
        out_shape=jax.ShapeDtypeStruct((m, n), BF16),
        grid=(m // bm, n // bn),
        in_specs=[
            pl.BlockSpec((bm, d), lambda i, j: (i, 0)),
            pl.BlockSpec((d, bn), lambda i, j: (0, j)),
            pl.BlockSpec((1, 3, bm, LANES), lambda i, j: (kind(j), 0, i % spt, 0)),
        ],
        out_specs=pl.BlockSpec((bm, bn), lambda i, j: (i, j)),
        scratch_shapes=[pltpu.VMEM((bm, d), BF16)],
        compiler_params=_cparams(2, 56),
        name="proj_rope",
    )(x2, w_in, tabs)


def _softmax_step(s_t, v_t, m_ref, acc_ref):
    m_prev = m_ref[...]
    m_new = jnp.maximum(m_prev, jnp.max(s_t, axis=0, keepdims=True))
    alpha = jnp.exp2(m_prev - m_new)
    p = jnp.exp2(s_t - m_new)
    acc_ref[...] = acc_ref[...] * alpha + jnp.dot(v_t, p.astype(v_t.dtype), preferred_element_type=F32)
    m_ref[...] = m_new


def _flash_pipeline(n_full, j_diag, diag_scores, full_scores, vt_ref, s_ref, slots, m_ref, acc_ref):
    sa, sb = slots
    m_ref[...] = jnp.full(m_ref.shape, NEG_INF, F32)
    acc_ref[...] = jnp.zeros(acc_ref.shape, F32)
    s_ref[sa] = diag_scores()
    n_pairs = (n_full + 1) // 2

    def pair(p, carry):
        s_ref[sb] = full_scores(2 * p)
        _softmax_step(s_ref[sa], vt_ref[jnp.where(p == 0, j_diag, 2 * p - 1)], m_ref, acc_ref)
        s_ref[sa] = full_scores(jnp.minimum(2 * p + 1, max(n_full - 1, 0)))
        _softmax_step(s_ref[sb], vt_ref[2 * p], m_ref, acc_ref)
        return carry

    def two_pairs(pp, carry):
        return pair(2 * pp + 1, pair(2 * pp, carry))

    if n_pairs >= 2:
        lax.fori_loop(0, n_pairs // 2, two_pairs, 0)
    if n_pairs % 2 == 1:
        pair(jnp.int32(n_pairs - 1), 0)
    if (n_full + 1) % 2 == 1:
        _softmax_step(s_ref[sa], vt_ref[j_diag if n_full == 0 else n_full - 1], m_ref, acc_ref)


def _store_v_transposed(v_ref, vt_ref, tk):
    pad = VT_ROWS - HEAD_DIM
    ones_row = (lax.broadcasted_iota(jnp.int32, (pad, tk), 0) == 0).astype(vt_ref.dtype)
    for j in range(vt_ref.shape[0]):
        vt_ref[j, 0:HEAD_DIM, :] = v_ref[j * tk:(j + 1) * tk, :].T
        vt_ref[j, HEAD_DIM:VT_ROWS, :] = ones_row


def _normalized(acc_ref):
    return acc_ref[0:HEAD_DIM, :] / acc_ref[HEAD_DIM:HEAD_DIM + 1, :]


def _diff_attn_kernel(lq1_ref, lk1_ref, lq2_ref, lk2_ref, g_ref, q_ref, k_ref, v_ref, o_ref,
                      qs_ref, vt_ref, s_ref, m_ref, acc_ref, *, tq, tk, lam_init):
    seq = q_ref.shape[0]
    _store_v_transposed(v_ref, vt_ref, tk)
    lam = (jnp.exp(jnp.sum(lq1_ref[...] * lk1_ref[...])) - jnp.exp(jnp.sum(lq2_ref[...] * lk2_ref[...]))
           + lam_init)

    for i in range(seq // tq):
        par = i % 2
        qs, m_par, acc_par = qs_ref.at[par], m_ref.at[par], acc_ref.at[par]
        q = q_ref[i * tq:(i + 1) * tq, :]
        lane = lax.broadcasted_iota(jnp.int32, q.shape, 1)
        zero = jnp.zeros_like(q)
        qs[0:tq, :] = jnp.where(lane < DIFF_SUB, q, zero)
        qs[tq:2 * tq, :] = jnp.where(lane >= DIFF_SUB, q, zero)

        def scores(j, qs=qs):
            k = k_ref[pl.ds(pl.multiple_of(j * tk, tk), tk), :]
            return lax.dot_general(k, qs[...], _NT, preferred_element_type=F32)

        j_diag = (i * tq) // tk

        def diag_scores(i=i, j_diag=j_diag, scores=scores):
            s_t = scores(j_diag)
            kpos = j_diag * tk + lax.broadcasted_iota(jnp.int32, s_t.shape, 0)
            qpos = i * tq + (lax.broadcasted_iota(jnp.int32, s_t.shape, 1) & (tq - 1))
            return jnp.where(kpos <= qpos, s_t, NEG_INF)

        _flash_pipeline(j_diag, j_diag, diag_scores, scores, vt_ref, s_ref, (2 * par, 2 * par + 1),
                        m_par, acc_par)

        o_t = _normalized(acc_par)
        o_t = o_t[:, 0:tq] - lam * o_t[:, tq:2 * tq]
        o_t = o_t * lax.rsqrt(jnp.mean(jnp.square(o_t), axis=0, keepdims=True) + RMS_EPS)
        o_t = o_t * g_ref[...] * (1.0 - lam_init)
        o_ref[i * tq:(i + 1) * tq, :] = o_t.T.astype(o_ref.dtype)


def _diff_attn(proj, lam_params, subln_g, batch, seq, n_heads, q_col, k_col, v_col, lam_init):
    tq = min(512, seq)
    tk = min(512, seq)
    assert seq % tk == 0 and tk % tq == 0 and (tq & (tq - 1)) == 0
    small = lambda n: pl.BlockSpec((1, n), lambda b, h: (0, 0))
    head = lambda col: pl.BlockSpec((seq, HEAD_DIM), lambda b, h: (b, col + h))
    kern = functools.partial(_diff_attn_kernel, tq=tq, tk=tk, lam_init=lam_init)
    return pl.pallas_call(
        kern,
        out_shape=jax.ShapeDtypeStruct((batch * seq, n_heads * HEAD_DIM), BF16),
        grid=(batch, n_heads),
        in_specs=[small(DIFF_SUB)] * 4 + [
            pl.BlockSpec((HEAD_DIM, 1), lambda b, h: (0, 0)),
            head(q_col), head(k_col), head(v_col),
        ],
        out_specs=head(0),
        scratch_shapes=[
            pltpu.VMEM((2, 2 * tq, HEAD_DIM), BF16),
            pltpu.VMEM((seq // tk, VT_ROWS, tk), BF16),
            pltpu.VMEM((4, tk, 2 * tq), F32),
            pltpu.VMEM((2, 1, 2 * tq), F32),
            pltpu.VMEM((2, VT_ROWS, 2 * tq), F32),
        ],
        compiler_params=_cparams(2, 40),
        name="diff_attn",
    )(*lam_params, subln_g, proj, proj, proj)


def _moba_attn_kernel(q_ref, k_ref, v_ref, o_ref, kmh_ref, kml_ref, vt_ref, sel_ref, s_ref, m_ref, acc_ref,
                      *, n_blocks, t):
    seq = q_ref.shape[0]
    blk = MOBA_BLOCK
    per = t // blk
    nbp = kmh_ref.shape[0]

    _store_v_transposed(v_ref, vt_ref, t)
    kmean = jnp.sum(k_ref[...].astype(F32).reshape(n_blocks, blk, HEAD_DIM), axis=1) * (1.0 / blk)
    if n_blocks < nbp:
        kmean = jnp.concatenate([kmean, jnp.zeros((nbp - n_blocks, HEAD_DIM), F32)], axis=0)
    hi = kmean.astype(BF16)
    kmh_ref[...] = hi
    kml_ref[...] = (kmean - hi.astype(F32)).astype(BF16)

    q_all = q_ref[...]
    gate = (lax.dot_general(kmh_ref[...], q_all, _NT, preferred_element_type=F32)
            + lax.dot_general(kml_ref[...], q_all, _NT, preferred_element_type=F32))
    blk_id = lax.broadcasted_iota(jnp.int32, (nbp, seq), 0)
    q_blk = lax.broadcasted_iota(jnp.int32, (nbp, seq), 1) // blk
    blk_f = blk_id.astype(F32)
    gate = jnp.where(blk_id < q_blk, gate, NEG_INF)
    sel = jnp.zeros((nbp, seq), F32)
    for _ in range(min(MOBA_TOPK, n_blocks)):
        mx = jnp.max(gate, axis=0, keepdims=True)
        cand = jnp.where(jnp.logical_and(gate == mx, gate > NEG_INF), blk_f, float(nbp))
        pick = blk_f == jnp.min(cand, axis=0, keepdims=True)
        sel = jnp.where(pick, 1.0, sel)
        gate = jnp.where(pick, NEG_INF, gate)
    sel_ref[...] = sel

    for i in range(seq // t):
        par = i % 2
        m_par, acc_par = m_ref.at[par], acc_ref.at[par]
        cols = slice(i * t, (i + 1) * t)

        def scores(g, cols=cols):
            k = k_ref[pl.ds(pl.multiple_of(g * t, t), t), :]
            return lax.dot_general(k, q_ref[cols, :], _NT, preferred_element_type=F32)

        def picked(g, cols=cols):
            rows = [jnp.broadcast_to(sel_ref[pl.ds(g * per + c, 1), cols] > 0.5, (blk, t))
                    for c in range(per)]
            return rows[0] if per == 1 else jnp.concatenate(rows, axis=0)

        def past_scores(g, scores=scores, picked=picked):
            return jnp.where(picked(g), scores(g), NEG_INF)

        def diag_scores(i=i, scores=scores, picked=picked):
            r = lax.broadcasted_iota(jnp.int32, (t, t), 0)
            c = lax.broadcasted_iota(jnp.int32, (t, t), 1)
            own_causal = jnp.logical_and(r // blk == c // blk, r <= c)
            return jnp.where(jnp.logical_or(own_causal, picked(i)), scores(i), NEG_INF)

        _flash_pipeline(i, i, diag_scores, past_scores, vt_ref, s_ref, (2 * par, 2 * par + 1),
                        m_par, acc_par)
        o_ref[cols, :] = _normalized(acc_par).T.astype(o_ref.dtype)


def _moba_attn(proj, batch, seq, n_heads, q_col, k_col, v_col):
    t = min(2 * MOBA_BLOCK, seq)
    assert seq % t == 0 and t % MOBA_BLOCK == 0
    nb = seq // MOBA_BLOCK
    nbp = -(-nb // BF16_SUBLANES) * BF16_SUBLANES
    nt = seq // t
    head = lambda col: pl.BlockSpec((seq, HEAD_DIM), lambda b, h: (b, col + h))
    kern = functools.partial(_moba_attn_kernel, n_blocks=nb, t=t)
    return pl.pallas_call(
        kern,
        out_shape=jax.ShapeDtypeStruct((batch * seq, n_heads * HEAD_DIM), BF16),
        grid=(batch, n_heads),
        in_specs=[head(q_col), head(k_col), head(v_col)],
        out_specs=head(0),
        scratch_shapes=[
            pltpu.VMEM((nbp, HEAD_DIM), BF16),
            pltpu.VMEM((nbp, HEAD_DIM), BF16),
            pltpu.VMEM((nt, VT_ROWS, t), BF16),
            pltpu.VMEM((nbp, seq), F32),
            pltpu.VMEM((4, t, t), F32),
            pltpu.VMEM((2, 1, t), F32),
            pltpu.VMEM((2, VT_ROWS, t), F32),
        ],
        compiler_params=_cparams(2, 40),
        name="moba_attn",
    )(proj, proj, proj)


def _layer_norm(y, g, b):
    mu = jnp.mean(y, axis=-1, keepdims=True)
    yc = y - mu
    var = jnp.mean(jnp.square(yc), axis=-1, keepdims=True)
    return yc * lax.rsqrt(var + LN_EPS) * g + b


def _out_ln1_kernel(a_ref, b_ref, wa_ref, wb_ref, x_ref, g_ref, beta_ref, h_ref, hb_ref):
    for ch in range(x_ref.shape[0] // ROW_CHUNK):
        rows = slice(ch * ROW_CHUNK, (ch + 1) * ROW_CHUNK)
        mix = (jnp.dot(a_ref[rows, :], wa_ref[...], preferred_element_type=F32)
               + jnp.dot(b_ref[rows, :], wb_ref[...], preferred_element_type=F32))
        h = _layer_norm(DEEPNORM_ALPHA * x_ref[rows, :] + mix, g_ref[...], beta_ref[...])
        h_ref[rows, :] = h
        hb_ref[rows, :] = h.astype(hb_ref.dtype)


def _out_ln1(a_out, b_out, w_out, x2, g, beta):
    m, d = x2.shape
    wa, wb = a_out.shape[1], b_out.shape[1]
    bm = 512
    assert m % bm == 0
    row = lambda w: pl.BlockSpec((bm, w), lambda i: (i, 0))
    vec = pl.BlockSpec((1, d), lambda i: (0, 0))
    return pl.pallas_call(
        _out_ln1_kernel,
        out_shape=(jax.ShapeDtypeStruct((m, d), F32), jax.ShapeDtypeStruct((m, d), BF16)),
        grid=(m // bm,),
        in_specs=[
            row(wa), row(wb),
            pl.BlockSpec((wa, d), lambda i: (0, 0)),
            pl.BlockSpec((wb, d), lambda i: (wa // wb, 0)),
            row(d), vec, vec,
        ],
        out_specs=(row(d), row(d)),
        compiler_params=_cparams(1, 56),
        name="out_ln1",
    )(a_out, b_out, w_out, w_out, x2, g, beta)


def _ffn_up_kernel(h_ref, wg32_ref, wv32_ref, cw_ref, cb_ref, o_ref, wg_ref, wv_ref, gbuf_ref,
                   *, bm, tiles_per_seq):
    r = pl.program_id(1)

    @pl.when(r == 0)
    def _():
        wg_ref[...] = wg32_ref[...].astype(BF16)
        wv_ref[...] = wv32_ref[...].astype(BF16)

    @pl.when(r % tiles_per_seq == 0)
    def _():
        gbuf_ref[0:CONV_HALO, :] = jnp.zeros((CONV_HALO, gbuf_ref.shape[1]), F32)

    cw = cw_ref[...]
    for c in range(bm // ROW_CHUNK):
        lo = c * ROW_CHUNK
        h = h_ref[lo:lo + ROW_CHUNK, :]
        g = jnp.dot(h, wg_ref[...], preferred_element_type=F32)
        val = jnp.dot(h, wv_ref[...], preferred_element_type=F32)
        gbuf_ref[CONV_HALO + lo:CONV_HALO + lo + ROW_CHUNK, :] = g
        g1 = gbuf_ref[pl.ds(CONV_HALO + lo - 1, ROW_CHUNK), :]
        g2 = gbuf_ref[pl.ds(CONV_HALO + lo - 2, ROW_CHUNK), :]
        gc = cb_ref[...] + (cw[0:1, :] * g2 + cw[1:2, :] * g1 + cw[2:3, :] * g)
        act = gc * jax.nn.sigmoid(gc) * val
        o_ref[lo:lo + ROW_CHUNK, :] = act.astype(o_ref.dtype)
    gbuf_ref[0:CONV_HALO, :] = gbuf_ref[bm:bm + CONV_HALO, :]


def _ffn_up(hb, w_up, conv_w, conv_b, seq):
    m, d = hb.shape
    f = conv_w.shape[1]
    bm = min(2048, seq)
    bn = 512
    assert seq % bm == 0 and f % bn == 0
    nct = f // bn
    kern = functools.partial(_ffn_up_kernel, bm=bm, tiles_per_seq=seq // bm)
    return pl.pallas_call(
        kern,
        out_shape=jax.ShapeDtypeStruct((m, f), BF16),
        grid=(nct, m // bm),
        in_specs=[
            pl.BlockSpec((bm, d), lambda c, r: (r, 0)),
            pl.BlockSpec((d, bn), lambda c, r: (0, c)),
            pl.BlockSpec((d, bn), lambda c, r: (0, nct + c)),
            pl.BlockSpec((CONV_W, bn), lambda c, r: (0, c)),
            pl.BlockSpec((1, bn), lambda c, r: (0, c)),
        ],
        out_specs=pl.BlockSpec((bm, bn), lambda c, r: (r, c)),
        scratch_shapes=[pltpu.VMEM((d, bn), BF16), pltpu.VMEM((d, bn), BF16),
                        pltpu.VMEM((bm + CONV_HALO, bn), F32)],
        compiler_params=_cparams(2, 54),
        name="ffn_up",
    )(hb, w_up, w_up, conv_w, conv_b)


def _ffn_down_kernel(a_ref, w_ref, h_ref, g_ref, beta_ref, o_ref):
    k = pl.program_id(1)
    last = pl.num_programs(1) - 1

    def chunks(step):
        for ch in range(o_ref.shape[0] // ROW_CHUNK):
            rows = slice(ch * ROW_CHUNK, (ch + 1) * ROW_CHUNK)
            step(rows, jnp.dot(a_ref[rows, :], w_ref[...], preferred_element_type=F32))

    def first(rows, part):
        o_ref[rows, :] = DEEPNORM_ALPHA * h_ref[rows, :] + part

    def middle(rows, part):
        o_ref[rows, :] += part

    def final(rows, part):
        o_ref[rows, :] = _layer_norm(o_ref[rows, :] + part, g_ref[...], beta_ref[...])

    pl.when(k == 0)(lambda: chunks(first))
    pl.when(jnp.logical_and(k > 0, k < last))(lambda: chunks(middle))
    pl.when(k == last)(lambda: chunks(final))


def _ffn_down(act, w_down, h1, g, beta):
    m, f = act.shape
    d = w_down.shape[1]
    bm, bk = 1024, 512
    assert m % bm == 0 and f % bk == 0 and f // bk >= 2
    vec = pl.BlockSpec((1, d), lambda i, k: (0, 0))
    return pl.pallas_call(
        _ffn_down_kernel,
        out_shape=jax.ShapeDtypeStruct((m, d), F32),
        grid=(m // bm, f // bk),
        in_specs=[
            pl.BlockSpec((bm, bk), lambda i, k: (i, k)),
            pl.BlockSpec((bk, d), lambda i, k: (k, 0)),
            pl.BlockSpec((bm, d), lambda i, k: (i, 0)),
            vec, vec,
        ],
        out_specs=pl.BlockSpec((bm, d), lambda i, k: (i, 0)),
        compiler_params=_cparams(2, 48),
        name="ffn_down",
    )(act, w_down, h1, g, beta)


def kernel(x, w_in, lambda_q1, lambda_k1, lambda_q2, lambda_k2, subln_g, w_out, ln1_g, ln1_b, w_up,
           conv_w, conv_b, w_down, ln2_g, ln2_b):
    batch, seq, d_model = x.shape
    depth = w_in.shape[0]
    mix_w = w_out.shape[1]
    diff_w = moba_w = mix_w // 2
    n_dh, n_mh = diff_w // HEAD_DIM, moba_w // HEAD_DIM
    assert w_in.shape[2] == 3 * diff_w + 3 * moba_w

    h = x.reshape(batch * seq, d_model)
    for l in range(depth):
        lam_init = 0.8 - 0.6 * math.exp(-0.3 * l)
        proj = _proj_rope(h, w_in[l].astype(BF16), seq, diff_w, moba_w)
        lam_params = [p[l].reshape(1, DIFF_SUB) for p in (lambda_q1, lambda_k1, lambda_q2, lambda_k2)]
        a_out = _diff_attn(proj, lam_params, subln_g[l].reshape(HEAD_DIM, 1), batch, seq, n_dh,
                           0, n_dh, 2 * n_dh, lam_init)
        b_out = _moba_attn(proj, batch, seq, n_mh, 3 * n_dh, 3 * n_dh + n_mh, 3 * n_dh + 2 * n_mh)
        h1, h1b = _out_ln1(a_out, b_out, w_out[l].astype(BF16), h,
                           ln1_g[l].reshape(1, d_model), ln1_b[l].reshape(1, d_model))
        act = _ffn_up(h1b, w_up[l], conv_w[l], conv_b[l].reshape(1, -1), seq)
        h = _ffn_down(act, w_down[l].astype(BF16), h1, ln2_g[l].reshape(1, d_model),
                      ln2_b[l].reshape(1, d_model))
    return h.reshape(batch, seq, d_model)
```

```python
import functools
import math

import jax
import jax.numpy as jnp
from jax import lax
from jax.experimental import pallas as pl
from jax.experimental.pallas import tpu as pltpu

F32 = jnp.float32
BF16 = jnp.bfloat16

LANES = 128
BF16_SUBLANES = 16
HEAD_DIM = 128
VT_ROWS = HEAD_DIM + BF16_SUBLANES
LOG2E = 1.4426950408889634
DIFF_SUB = HEAD_DIM // 2
ROPE_THETA = 500000.0
ROT_FRACTION = 4
MOBA_BLOCK = 256
MOBA_TOPK = 3
CONV_W = 3
CONV_HALO = 8
ROW_CHUNK = 256
LN_EPS = 1e-5
RMS_EPS = 1e-5
DEPTH = 1
DEEPNORM_ALPHA = (2.0 * DEPTH) ** 0.25
DIFF_SCALE = DIFF_SUB ** -0.5
MOBA_SCALE = HEAD_DIM ** -0.5
NEG_INF = float("-inf")

_NT = (((1,), (1,)), ((), ()))


def _cparams(n_axes, vmem_mib):
    return pltpu.CompilerParams(
        dimension_semantics=("arbitrary",) * n_axes,
        vmem_limit_bytes=vmem_mib * 1024 * 1024,
    )


def _rope_coeffs(seq, period):
    rot = period // ROT_FRACTION
    half = rot // 2
    inv = 1.0 / (ROPE_THETA ** (jnp.arange(0, rot, 2, dtype=F32) / rot))
    dim = [l % period for l in range(LANES)]
    lane_inv = inv[jnp.asarray([d % half for d in dim])]
    first = jnp.asarray([d < half for d in dim])[None, :]
    second = jnp.asarray([half <= d < rot for d in dim])[None, :]
    ang = jnp.arange(seq, dtype=F32)[:, None] * lane_inv[None, :]
    cos, sin = jnp.cos(ang), jnp.sin(ang)
    c = jnp.where(jnp.logical_or(first, second), cos, 1.0)
    s1 = jnp.where(first, -sin, 0.0)
    s2 = jnp.where(second, sin, 0.0)
    return jnp.stack([c, s1, s2], axis=0)


def _proj_rope_kernel(x_ref, w_ref, tab_ref, *refs, n_side, diff_tiles, moba_tiles, q_scales, n_heads_blk):
    side_in, (o_ref, *side_out), xb_ref = refs[:n_side], refs[n_side:2 * n_side + 1], refs[-1]
    j = pl.program_id(1)
    scale = jnp.float32(1.0)
    for t, s in q_scales:
        scale = jnp.where(j == t, jnp.float32(s), scale)

    def tile(half, first):
        for ch in range(x_ref.shape[0] // ROW_CHUNK):
            rows = slice(ch * ROW_CHUNK, (ch + 1) * ROW_CHUNK)
            if first:
                xb_ref[rows, :] = x_ref[rows, :].astype(BF16)
            acc = jnp.dot(xb_ref[rows, :], w_ref[...], preferred_element_type=F32)
            if half is None:
                o_ref[rows, :] = acc.astype(o_ref.dtype)
                continue
            c, s1, s2 = tab_ref[0, 0, rows, :], tab_ref[0, 1, rows, :], tab_ref[0, 2, rows, :]
            for hb in range(n_heads_blk):
                a = acc[:, hb * LANES:(hb + 1) * LANES]
                o = a * c + pltpu.roll(a, LANES - half, 1) * s1 + pltpu.roll(a, half, 1) * s2
                o_ref[rows, hb * LANES:(hb + 1) * LANES] = (o * scale).astype(o_ref.dtype)

    assert diff_tiles[0] == 0
    is_diff = functools.reduce(jnp.logical_or, [j == t for t in diff_tiles[1:]])
    is_moba = functools.reduce(jnp.logical_or, [j == t for t in moba_tiles])
    diff_half = DIFF_SUB // ROT_FRACTION // 2
    moba_half = HEAD_DIM // ROT_FRACTION // 2

    @pl.when(j == 0)
    def _():
        for src, dst in zip(side_in, side_out):
            dst[...] = src[...].astype(dst.dtype)
        tile(diff_half, True)

    @pl.when(is_diff)
    def _():
        tile(diff_half, False)

    @pl.when(is_moba)
    def _():
        tile(moba_half, False)

    @pl.when(jnp.logical_not(jnp.logical_or(j == 0, jnp.logical_or(is_diff, is_moba))))
    def _():
        tile(None, False)


def _proj_rope(x2, w_in, seq, diff_w, moba_w, side_weights):
    m, d = x2.shape
    n = w_in.shape[1]
    bm = min(1024, seq)
    bn = 1024
    assert seq % bm == 0 and m % bm == 0 and diff_w % bn == 0 and moba_w % bn == 0
    dt, mt = diff_w // bn, moba_w // bn
    q_d_tiles = list(range(0, dt))
    k_d_tiles = list(range(dt, 2 * dt))
    q_m_tiles = list(range(3 * dt, 3 * dt + mt))
    k_m_tiles = list(range(3 * dt + mt, 3 * dt + 2 * mt))
    tabs = jnp.stack([_rope_coeffs(seq, DIFF_SUB), _rope_coeffs(seq, HEAD_DIM)], axis=0)

    def kind(j):
        k = jnp.int32(0)
        for t in q_m_tiles + k_m_tiles:
            k = jnp.where(j == t, 1, k)
        return k

    spt = seq // bm
    n_row_tiles = m // bm
    side_specs = []
    for w in side_weights:
        slab = w.shape[0] // n_row_tiles
        assert slab * n_row_tiles == w.shape[0] and slab % BF16_SUBLANES == 0
        side_specs.append(pl.BlockSpec((slab, w.shape[1]), lambda i, j: (i, 0)))
    q_scales = ([(t, DIFF_SCALE * LOG2E) for t in q_d_tiles] + [(t, MOBA_SCALE * LOG2E) for t in q_m_tiles])
    kern = functools.partial(_proj_rope_kernel, n_side=len(side_weights), diff_tiles=q_d_tiles + k_d_tiles,
                             moba_tiles=q_m_tiles + k_m_tiles, q_scales=q_scales, n_heads_blk=bn // LANES)
    proj, *side_bf16 = pl.pallas_call(
        kern,
        out_shape=[jax.ShapeDtypeStruct((m, n), BF16)]
        + [jax.ShapeDtypeStruct(w.shape, BF16) for w in side_weights],
        grid=(n_row_tiles, n // bn),
        in_specs=[
            pl.BlockSpec((bm, d), lambda i, j: (i, 0)),
            pl.BlockSpec((d, bn), lambda i, j: (0, j)),
            pl.BlockSpec((1, 3, bm, LANES), lambda i, j: (kind(j), 0, i % spt, 0)),
        ] + side_specs,
        out_specs=[pl.BlockSpec((bm, bn), lambda i, j: (i, j))] + side_specs,
        scratch_shapes=[pltpu.VMEM((bm, d), BF16)],
        compiler_params=_cparams(2, 56),
        name="proj_rope",
    )(x2, w_in, tabs, *side_weights)
    return proj, side_bf16


def _softmax_step(s_t, v_t, m_ref, acc_ref):
    m_prev = m_ref[...]
    m_new = jnp.maximum(m_prev, jnp.max(s_t, axis=0, keepdims=True))
    alpha = jnp.exp2(m_prev - m_new)
    p = jnp.exp2(s_t - m_new)
    acc_ref[...] = acc_ref[...] * alpha + jnp.dot(v_t, p.astype(v_t.dtype), preferred_element_type=F32)
    m_ref[...] = m_new


def _flash_pipeline(n_full, j_diag, diag_scores, full_scores, vt_ref, s_ref, slots, m_ref, acc_ref):
    sa, sb = slots
    m_ref[...] = jnp.full(m_ref.shape, NEG_INF, F32)
    acc_ref[...] = jnp.zeros(acc_ref.shape, F32)
    s_ref[sa] = diag_scores()
    n_pairs = (n_full + 1) // 2

    def pair(p, carry):
        s_ref[sb] = full_scores(2 * p)
        _softmax_step(s_ref[sa], vt_ref[jnp.where(p == 0, j_diag, 2 * p - 1)], m_ref, acc_ref)
        s_ref[sa] = full_scores(jnp.minimum(2 * p + 1, max(n_full - 1, 0)))
        _softmax_step(s_ref[sb], vt_ref[2 * p], m_ref, acc_ref)
        return carry

    def two_pairs(pp, carry):
        return pair(2 * pp + 1, pair(2 * pp, carry))

    if n_pairs >= 2:
        lax.fori_loop(0, n_pairs // 2, two_pairs, 0)
    if n_pairs % 2 == 1:
        pair(jnp.int32(n_pairs - 1), 0)
    if (n_full + 1) % 2 == 1:
        _softmax_step(s_ref[sa], vt_ref[j_diag if n_full == 0 else n_full - 1], m_ref, acc_ref)


def _store_v_transposed(v_ref, vt_ref, tk):
    pad = VT_ROWS - HEAD_DIM
    ones_row = (lax.broadcasted_iota(jnp.int32, (pad, tk), 0) == 0).astype(vt_ref.dtype)
    for j in range(vt_ref.shape[0]):
        vt_ref[j, 0:HEAD_DIM, :] = v_ref[j * tk:(j + 1) * tk, :].T
        vt_ref[j, HEAD_DIM:VT_ROWS, :] = ones_row


def _normalized(acc_ref):
    return acc_ref[0:HEAD_DIM, :] / acc_ref[HEAD_DIM:HEAD_DIM + 1, :]


def _diff_attn_kernel(lq1_ref, lk1_ref, lq2_ref, lk2_ref, g_ref, q_ref, k_ref, v_ref, o_ref,
                      qs_ref, vt_ref, s_ref, m_ref, acc_ref, *, tq, tk, lam_init):
    seq = q_ref.shape[0]
    _store_v_transposed(v_ref, vt_ref, tk)
    lam = (jnp.exp(jnp.sum(lq1_ref[...] * lk1_ref[...])) - jnp.exp(jnp.sum(lq2_ref[...] * lk2_ref[...]))
           + lam_init)

    for i in range(seq // tq):
        par = i % 2
        qs, m_par, acc_par = qs_ref.at[par], m_ref.at[par], acc_ref.at[par]
        q = q_ref[i * tq:(i + 1) * tq, :]
        lane = lax.broadcasted_iota(jnp.int32, q.shape, 1)
        zero = jnp.zeros_like(q)
        qs[0:tq, :] = jnp.where(lane < DIFF_SUB, q, zero)
        qs[tq:2 * tq, :] = jnp.where(lane >= DIFF_SUB, q, zero)

        def scores(j, qs=qs):
            k = k_ref[pl.ds(pl.multiple_of(j * tk, tk), tk), :]
            return lax.dot_general(k, qs[...], _NT, preferred_element_type=F32)

        j_diag = (i * tq) // tk

        def diag_scores(i=i, j_diag=j_diag, scores=scores):
            s_t = scores(j_diag)
            kpos = j_diag * tk + lax.broadcasted_iota(jnp.int32, s_t.shape, 0)
            qpos = i * tq + (lax.broadcasted_iota(jnp.int32, s_t.shape, 1) & (tq - 1))
            return jnp.where(kpos <= qpos, s_t, NEG_INF)

        _flash_pipeline(j_diag, j_diag, diag_scores, scores, vt_ref, s_ref, (2 * par, 2 * par + 1),
                        m_par, acc_par)

        o_t = _normalized(acc_par)
        o_t = o_t[:, 0:tq] - lam * o_t[:, tq:2 * tq]
        o_t = o_t * lax.rsqrt(jnp.mean(jnp.square(o_t), axis=0, keepdims=True) + RMS_EPS)
        o_t = o_t * g_ref[...] * (1.0 - lam_init)
        o_ref[i * tq:(i + 1) * tq, :] = o_t.T.astype(o_ref.dtype)


def _diff_attn(proj, lam_params, subln_g, batch, seq, n_heads, q_col, k_col, v_col, lam_init):
    tq = min(512, seq)
    tk = min(512, seq)
    assert seq % tk == 0 and tk % tq == 0 and (tq & (tq - 1)) == 0
    small = lambda n: pl.BlockSpec((1, n), lambda b, h: (0, 0))
    head = lambda col: pl.BlockSpec((seq, HEAD_DIM), lambda b, h: (b, col + h))
    kern = functools.partial(_diff_attn_kernel, tq=tq, tk=tk, lam_init=lam_init)
    return pl.pallas_call(
        kern,
        out_shape=jax.ShapeDtypeStruct((batch * seq, n_heads * HEAD_DIM), BF16),
        grid=(batch, n_heads),
        in_specs=[small(DIFF_SUB)] * 4 + [
            pl.BlockSpec((HEAD_DIM, 1), lambda b, h: (0, 0)),
            head(q_col), head(k_col), head(v_col),
        ],
        out_specs=head(0),
        scratch_shapes=[
            pltpu.VMEM((2, 2 * tq, HEAD_DIM), BF16),
            pltpu.VMEM((seq // tk, VT_ROWS, tk), BF16),
            pltpu.VMEM((4, tk, 2 * tq), F32),
            pltpu.VMEM((2, 1, 2 * tq), F32),
            pltpu.VMEM((2, VT_ROWS, 2 * tq), F32),
        ],
        compiler_params=_cparams(2, 40),
        name="diff_attn",
    )(*lam_params, subln_g, proj, proj, proj)


def _moba_attn_kernel(q_ref, k_ref, v_ref, o_ref, kmh_ref, kml_ref, vt_ref, sel_ref, s_ref, m_ref, acc_ref,
                      *, n_blocks, t):
    seq = q_ref.shape[0]
    blk = MOBA_BLOCK
    per = t // blk
    nbp = kmh_ref.shape[0]

    _store_v_transposed(v_ref, vt_ref, t)
    kmean = jnp.sum(k_ref[...].astype(F32).reshape(n_blocks, blk, HEAD_DIM), axis=1) * (1.0 / blk)
    if n_blocks < nbp:
        kmean = jnp.concatenate([kmean, jnp.zeros((nbp - n_blocks, HEAD_DIM), F32)], axis=0)
    hi = kmean.astype(BF16)
    kmh_ref[...] = hi
    kml_ref[...] = (kmean - hi.astype(F32)).astype(BF16)

    q_all = q_ref[...]
    gate = (lax.dot_general(kmh_ref[...], q_all, _NT, preferred_element_type=F32)
            + lax.dot_general(kml_ref[...], q_all, _NT, preferred_element_type=F32))
    blk_id = lax.broadcasted_iota(jnp.int32, (nbp, seq), 0)
    q_blk = lax.broadcasted_iota(jnp.int32, (nbp, seq), 1) // blk
    blk_f = blk_id.astype(F32)
    gate = jnp.where(blk_id < q_blk, gate, NEG_INF)
    sel = jnp.zeros((nbp, seq), F32)
    for _ in range(min(MOBA_TOPK, n_blocks)):
        mx = jnp.max(gate, axis=0, keepdims=True)
        cand = jnp.where(jnp.logical_and(gate == mx, gate > NEG_INF), blk_f, float(nbp))
        pick = blk_f == jnp.min(cand, axis=0, keepdims=True)
        sel = jnp.where(pick, 1.0, sel)
        gate = jnp.where(pick, NEG_INF, gate)
    sel_ref[...] = sel

    for i in range(seq // t):
        par = i % 2
        m_par, acc_par = m_ref.at[par], acc_ref.at[par]
        cols = slice(i * t, (i + 1) * t)

        def scores(g, cols=cols):
            k = k_ref[pl.ds(pl.multiple_of(g * t, t), t), :]
            return lax.dot_general(k, q_ref[cols, :], _NT, preferred_element_type=F32)

        def picked(g, cols=cols):
            rows = [jnp.broadcast_to(sel_ref[pl.ds(g * per + c, 1), cols] > 0.5, (blk, t))
                    for c in range(per)]
            return rows[0] if per == 1 else jnp.concatenate(rows, axis=0)

        def past_scores(g, scores=scores, picked=picked):
            return jnp.where(picked(g), scores(g), NEG_INF)

        def diag_scores(i=i, scores=scores, picked=picked):
            r = lax.broadcasted_iota(jnp.int32, (t, t), 0)
            c = lax.broadcasted_iota(jnp.int32, (t, t), 1)
            own_causal = jnp.logical_and(r // blk == c // blk, r <= c)
            return jnp.where(jnp.logical_or(own_causal, picked(i)), scores(i), NEG_INF)

        _flash_pipeline(i, i, diag_scores, past_scores, vt_ref, s_ref, (2 * par, 2 * par + 1),
                        m_par, acc_par)
        o_ref[cols, :] = _normalized(acc_par).T.astype(o_ref.dtype)


def _moba_attn(proj, batch, seq, n_heads, q_col, k_col, v_col):
    t = min(2 * MOBA_BLOCK, seq)
    assert seq % t == 0 and t % MOBA_BLOCK == 0
    nb = seq // MOBA_BLOCK
    nbp = -(-nb // BF16_SUBLANES) * BF16_SUBLANES
    nt = seq // t
    head = lambda col: pl.BlockSpec((seq, HEAD_DIM), lambda b, h: (b, col + h))
    kern = functools.partial(_moba_attn_kernel, n_blocks=nb, t=t)
    return pl.pallas_call(
        kern,
        out_shape=jax.ShapeDtypeStruct((batch * seq, n_heads * HEAD_DIM), BF16),
        grid=(batch, n_heads),
        in_specs=[head(q_col), head(k_col), head(v_col)],
        out_specs=head(0),
        scratch_shapes=[
            pltpu.VMEM((nbp, HEAD_DIM), BF16),
            pltpu.VMEM((nbp, HEAD_DIM), BF16),
            pltpu.VMEM((nt, VT_ROWS, t), BF16),
            pltpu.VMEM((nbp, seq), F32),
            pltpu.VMEM((4, t, t), F32),
            pltpu.VMEM((2, 1, t), F32),
            pltpu.VMEM((2, VT_ROWS, t), F32),
        ],
        compiler_params=_cparams(2, 40),
        name="moba_attn",
    )(proj, proj, proj)


def _layer_norm(y, g, b):
    mu = jnp.mean(y, axis=-1, keepdims=True)
    yc = y - mu
    var = jnp.mean(jnp.square(yc), axis=-1, keepdims=True)
    return yc * lax.rsqrt(var + LN_EPS) * g + b


def _out_ln1_kernel(a_ref, b_ref, wa_ref, wb_ref, x_ref, g_ref, beta_ref, h_ref, hb_ref):
    for ch in range(x_ref.shape[0] // ROW_CHUNK):
        rows = slice(ch * ROW_CHUNK, (ch + 1) * ROW_CHUNK)
        mix = (jnp.dot(a_ref[rows, :], wa_ref[...], preferred_element_type=F32)
               + jnp.dot(b_ref[rows, :], wb_ref[...], preferred_element_type=F32))
        h = _layer_norm(DEEPNORM_ALPHA * x_ref[rows, :] + mix, g_ref[...], beta_ref[...])
        h_ref[rows, :] = h
        hb_ref[rows, :] = h.astype(hb_ref.dtype)


def _out_ln1(a_out, b_out, w_out, x2, g, beta):
    m, d = x2.shape
    wa, wb = a_out.shape[1], b_out.shape[1]
    bm = 512
    assert m % bm == 0
    row = lambda w: pl.BlockSpec((bm, w), lambda i: (i, 0))
    vec = pl.BlockSpec((1, d), lambda i: (0, 0))
    return pl.pallas_call(
        _out_ln1_kernel,
        out_shape=(jax.ShapeDtypeStruct((m, d), F32), jax.ShapeDtypeStruct((m, d), BF16)),
        grid=(m // bm,),
        in_specs=[
            row(wa), row(wb),
            pl.BlockSpec((wa, d), lambda i: (0, 0)),
            pl.BlockSpec((wb, d), lambda i: (wa // wb, 0)),
            row(d), vec, vec,
        ],
        out_specs=(row(d), row(d)),
        compiler_params=_cparams(1, 56),
        name="out_ln1",
    )(a_out, b_out, w_out, w_out, x2, g, beta)


def _ffn_up_kernel(h_ref, wg32_ref, wv32_ref, cw_ref, cb_ref, o_ref, wg_ref, wv_ref, gbuf_ref,
                   *, bm, tiles_per_seq):
    r = pl.program_id(1)

    @pl.when(r == 0)
    def _():
        wg_ref[...] = wg32_ref[...].astype(BF16)
        wv_ref[...] = wv32_ref[...].astype(BF16)

    @pl.when(r % tiles_per_seq == 0)
    def _():
        gbuf_ref[0:CONV_HALO, :] = jnp.zeros((CONV_HALO, gbuf_ref.shape[1]), F32)

    cw = cw_ref[...]
    for c in range(bm // ROW_CHUNK):
        lo = c * ROW_CHUNK
        h = h_ref[lo:lo + ROW_CHUNK, :]
        g = jnp.dot(h, wg_ref[...], preferred_element_type=F32)
        val = jnp.dot(h, wv_ref[...], preferred_element_type=F32)
        gbuf_ref[CONV_HALO + lo:CONV_HALO + lo + ROW_CHUNK, :] = g
        g1 = gbuf_ref[pl.ds(CONV_HALO + lo - 1, ROW_CHUNK), :]
        g2 = gbuf_ref[pl.ds(CONV_HALO + lo - 2, ROW_CHUNK), :]
        gc = cb_ref[...] + (cw[0:1, :] * g2 + cw[1:2, :] * g1 + cw[2:3, :] * g)
        act = gc * jax.nn.sigmoid(gc) * val
        o_ref[lo:lo + ROW_CHUNK, :] = act.astype(o_ref.dtype)
    gbuf_ref[0:CONV_HALO, :] = gbuf_ref[bm:bm + CONV_HALO, :]


def _ffn_up(hb, w_up, conv_w, conv_b, seq):
    m, d = hb.shape
    f = conv_w.shape[1]
    bm = min(2048, seq)
    bn = 512
    assert seq % bm == 0 and f % bn == 0
    nct = f // bn
    kern = functools.partial(_ffn_up_kernel, bm=bm, tiles_per_seq=seq // bm)
    return pl.pallas_call(
        kern,
        out_shape=jax.ShapeDtypeStruct((m, f), BF16),
        grid=(nct, m // bm),
        in_specs=[
            pl.BlockSpec((bm, d), lambda c, r: (r, 0)),
            pl.BlockSpec((d, bn), lambda c, r: (0, c)),
            pl.BlockSpec((d, bn), lambda c, r: (0, nct + c)),
            pl.BlockSpec((CONV_W, bn), lambda c, r: (0, c)),
            pl.BlockSpec((1, bn), lambda c, r: (0, c)),
        ],
        out_specs=pl.BlockSpec((bm, bn), lambda c, r: (r, c)),
        scratch_shapes=[pltpu.VMEM((d, bn), BF16), pltpu.VMEM((d, bn), BF16),
                        pltpu.VMEM((bm + CONV_HALO, bn), F32)],
        compiler_params=_cparams(2, 54),
        name="ffn_up",
    )(hb, w_up, w_up, conv_w, conv_b)


def _ffn_down_kernel(a_ref, w_ref, h_ref, g_ref, beta_ref, o_ref):
    k = pl.program_id(1)
    last = pl.num_programs(1) - 1

    def chunks(step):
        for ch in range(o_ref.shape[0] // ROW_CHUNK):
            rows = slice(ch * ROW_CHUNK, (ch + 1) * ROW_CHUNK)
            step(rows, jnp.dot(a_ref[rows, :], w_ref[...], preferred_element_type=F32))

    def first(rows, part):
        o_ref[rows, :] = DEEPNORM_ALPHA * h_ref[rows, :] + part

    def middle(rows, part):
        o_ref[rows, :] += part

    def final(rows, part):
        o_ref[rows, :] = _layer_norm(o_ref[rows, :] + part, g_ref[...], beta_ref[...])

    pl.when(k == 0)(lambda: chunks(first))
    pl.when(jnp.logical_and(k > 0, k < last))(lambda: chunks(middle))
    pl.when(k == last)(lambda: chunks(final))


def _ffn_down(act, w_down, h1, g, beta):
    m, f = act.shape
    d = w_down.shape[1]
    bm, bk = 1024, 512
    assert m % bm == 0 and f % bk == 0 and f // bk >= 2
    vec = pl.BlockSpec((1, d), lambda i, k: (0, 0))
    return pl.pallas_call(
        _ffn_down_kernel,
        out_shape=jax.ShapeDtypeStruct((m, d), F32),
        grid=(m // bm, f // bk),
        in_specs=[
            pl.BlockSpec((bm, bk), lambda i, k: (i, k)),
            pl.BlockSpec((bk, d), lambda i, k: (k, 0)),
            pl.BlockSpec((bm, d), lambda i, k: (i, 0)),
            vec, vec,
        ],
        out_specs=pl.BlockSpec((bm, d), lambda i, k: (i, 0)),
        compiler_params=_cparams(2, 48),
        name="ffn_down",
    )(act, w_down, h1, g, beta)


def kernel(x, w_in, lambda_q1, lambda_k1, lambda_q2, lambda_k2, subln_g, w_out, ln1_g, ln1_b, w_up,
           conv_w, conv_b, w_down, ln2_g, ln2_b):
    batch, seq, d_model = x.shape
    depth = w_in.shape[0]
    mix_w = w_out.shape[1]
    diff_w = moba_w = mix_w // 2
    n_dh, n_mh = diff_w // HEAD_DIM, moba_w // HEAD_DIM
    assert w_in.shape[2] == 3 * diff_w + 3 * moba_w

    h = x.reshape(batch * seq, d_model)
    for l in range(depth):
        lam_init = 0.8 - 0.6 * math.exp(-0.3 * l)
        proj, (w_out_b, w_down_b) = _proj_rope(h, w_in[l].astype(BF16), seq, diff_w, moba_w,
                                               [w_out[l], w_down[l]])
        lam_params = [p[l].reshape(1, DIFF_SUB) for p in (lambda_q1, lambda_k1, lambda_q2, lambda_k2)]
        a_out = _diff_attn(proj, lam_params, subln_g[l].reshape(HEAD_DIM, 1), batch, seq, n_dh,
                           0, n_dh, 2 * n_dh, lam_init)
        b_out = _moba_attn(proj, batch, seq, n_mh, 3 * n_dh, 3 * n_dh + n_mh, 3 * n_dh + 2 * n_mh)
        h1, h1b = _out_ln1(a_out, b_out, w_out_b, h,
                           ln1_g[l].reshape(1, d_model), ln1_b[l].reshape(1, d_model))
        act = _ffn_up(h1b, w_up[l], conv_w[l], conv_b[l].reshape(1, -1), seq)
        h = _ffn_down(act, w_down_b, h1, ln2_g[l].reshape(1, d_model),
                      ln2_b[l].reshape(1, d_model))
    return h.reshape(batch, seq, d_model)
```

```python
import functools
import math

import jax
import jax.numpy as jnp
from jax import lax
from jax.experimental import pallas as pl
from jax.experimental.pallas import tpu as pltpu

F32 = jnp.float32
BF16 = jnp.bfloat16

LANES = 128
BF16_SUBLANES = 16
HEAD_DIM = 128
VT_ROWS = HEAD_DIM + BF16_SUBLANES
LOG2E = 1.4426950408889634
DIFF_SUB = HEAD_DIM // 2
ROPE_THETA = 500000.0
ROT_FRACTION = 4
MOBA_BLOCK = 256
MOBA_TOPK = 3
CONV_W = 3
CONV_HALO = 8
ROW_CHUNK = 256
LN_EPS = 1e-5
RMS_EPS = 1e-5
DEPTH = 1
DEEPNORM_ALPHA = (2.0 * DEPTH) ** 0.25
DIFF_SCALE = DIFF_SUB ** -0.5
MOBA_SCALE = HEAD_DIM ** -0.5
NEG_INF = float("-inf")

_NT = (((1,), (1,)), ((), ()))


def _cparams(n_axes, vmem_mib):
    return pltpu.CompilerParams(
        dimension_semantics=("arbitrary",) * n_axes,
        vmem_limit_bytes=vmem_mib * 1024 * 1024,
    )


def _rope_coeffs(seq, period):
    rot = period // ROT_FRACTION
    half = rot // 2
    inv = 1.0 / (ROPE_THETA ** (jnp.arange(0, rot, 2, dtype=F32) / rot))
    dim = [l % period for l in range(LANES)]
    lane_inv = inv[jnp.asarray([d % half for d in dim])]
    first = jnp.asarray([d < half for d in dim])[None, :]
    second = jnp.asarray([half <= d < rot for d in dim])[None, :]
    ang = jnp.arange(seq, dtype=F32)[:, None] * lane_inv[None, :]
    cos, sin = jnp.cos(ang), jnp.sin(ang)
    c = jnp.where(jnp.logical_or(first, second), cos, 1.0)
    s1 = jnp.where(first, -sin, 0.0)
    s2 = jnp.where(second, sin, 0.0)
    return jnp.stack([c, s1, s2], axis=0)


def _proj_rope_kernel(x_ref, w_ref, tab_ref, o_ref, xb_ref, *, diff_tiles, moba_tiles, q_scales,
                      n_heads_blk):
    j = pl.program_id(1)
    scale = jnp.float32(1.0)
    for t, s in q_scales:
        scale = jnp.where(j == t, jnp.float32(s), scale)

    def tile(half, first):
        for ch in range(x_ref.shape[0] // ROW_CHUNK):
            rows = slice(ch * ROW_CHUNK, (ch + 1) * ROW_CHUNK)
            if first:
                xb_ref[rows, :] = x_ref[rows, :].astype(BF16)
            acc = jnp.dot(xb_ref[rows, :], w_ref[...], preferred_element_type=F32)
            if half is None:
                o_ref[rows, :] = acc.astype(o_ref.dtype)
                continue
            c, s1, s2 = tab_ref[0, 0, rows, :], tab_ref[0, 1, rows, :], tab_ref[0, 2, rows, :]
            for hb in range(n_heads_blk):
                a = acc[:, hb * LANES:(hb + 1) * LANES]
                o = a * c + pltpu.roll(a, LANES - half, 1) * s1 + pltpu.roll(a, half, 1) * s2
                o_ref[rows, hb * LANES:(hb + 1) * LANES] = (o * scale).astype(o_ref.dtype)

    assert diff_tiles[0] == 0
    is_diff = functools.reduce(jnp.logical_or, [j == t for t in diff_tiles[1:]])
    is_moba = functools.reduce(jnp.logical_or, [j == t for t in moba_tiles])
    diff_half = DIFF_SUB // ROT_FRACTION // 2
    moba_half = HEAD_DIM // ROT_FRACTION // 2

    @pl.when(j == 0)
    def _():
        tile(diff_half, True)

    @pl.when(is_diff)
    def _():
        tile(diff_half, False)

    @pl.when(is_moba)
    def _():
        tile(moba_half, False)

    @pl.when(jnp.logical_not(jnp.logical_or(j == 0, jnp.logical_or(is_diff, is_moba))))
    def _():
        tile(None, False)


def _proj_rope(x2, w_in, seq, diff_w, moba_w):
    m, d = x2.shape
    n = w_in.shape[1]
    bm = min(1024, seq)
    bn = 1024
    assert seq % bm == 0 and m % bm == 0 and diff_w % bn == 0 and moba_w % bn == 0
    dt, mt = diff_w // bn, moba_w // bn
    q_d_tiles = list(range(0, dt))
    k_d_tiles = list(range(dt, 2 * dt))
    q_m_tiles = list(range(3 * dt, 3 * dt + mt))
    k_m_tiles = list(range(3 * dt + mt, 3 * dt + 2 * mt))
    tabs = jnp.stack([_rope_coeffs(seq, DIFF_SUB), _rope_coeffs(seq, HEAD_DIM)], axis=0)

    def kind(j):
        k = jnp.int32(0)
        for t in q_m_tiles + k_m_tiles:
            k = jnp.where(j == t, 1, k)
        return k

    spt = seq // bm
    q_scales = ([(t, DIFF_SCALE * LOG2E) for t in q_d_tiles] + [(t, MOBA_SCALE * LOG2E) for t in q_m_tiles])
    kern = functools.partial(_proj_rope_kernel, diff_tiles=q_d_tiles + k_d_tiles,
                             moba_tiles=q_m_tiles + k_m_tiles, q_scales=q_scales, n_heads_blk=bn // LANES)
    return pl.pallas_call(
        kern,
        out_shape=jax.ShapeDtypeStruct((m, n), BF16),
        grid=(m // bm, n // bn),
        in_specs=[
            pl.BlockSpec((bm, d), lambda i, j: (i, 0)),
            pl.BlockSpec((d, bn), lambda i, j: (0, j)),
            pl.BlockSpec((1, 3, bm, LANES), lambda i, j: (kind(j), 0, i % spt, 0)),
        ],
        out_specs=pl.BlockSpec((bm, bn), lambda i, j: (i, j)),
        scratch_shapes=[pltpu.VMEM((bm, d), BF16)],
        compiler_params=_cparams(2, 56),
        name="proj_rope",
    )(x2, w_in, tabs)


def _softmax_step(s_t, v_t, m_ref, acc_ref):
    m_prev = m_ref[...]
    m_new = jnp.maximum(m_prev, jnp.max(s_t, axis=0, keepdims=True))
    alpha = jnp.exp2(m_prev - m_new)
    p = jnp.exp2(s_t - m_new)
    acc_ref[...] = acc_ref[...] * alpha + jnp.dot(v_t, p.astype(v_t.dtype), preferred_element_type=F32)
    m_ref[...] = m_new


def _flash_pipeline(n_full, j_diag, diag_scores, full_scores, vt_ref, s_ref, slots, m_ref, acc_ref):
    sa, sb = slots
    m_ref[...] = jnp.full(m_ref.shape, NEG_INF, F32)
    acc_ref[...] = jnp.zeros(acc_ref.shape, F32)
    s_ref[sa] = diag_scores()
    n_pairs = (n_full + 1) // 2

    def pair(p, carry):
        s_ref[sb] = full_scores(2 * p)
        _softmax_step(s_ref[sa], vt_ref[jnp.where(p == 0, j_diag, 2 * p - 1)], m_ref, acc_ref)
        s_ref[sa] = full_scores(jnp.minimum(2 * p + 1, max(n_full - 1, 0)))
        _softmax_step(s_ref[sb], vt_ref[2 * p], m_ref, acc_ref)
        return carry

    def two_pairs(pp, carry):
        return pair(2 * pp + 1, pair(2 * pp, carry))

    if n_pairs >= 2:
        lax.fori_loop(0, n_pairs // 2, two_pairs, 0)
    if n_pairs % 2 == 1:
        pair(jnp.int32(n_pairs - 1), 0)
    if (n_full + 1) % 2 == 1:
        _softmax_step(s_ref[sa], vt_ref[j_diag if n_full == 0 else n_full - 1], m_ref, acc_ref)


def _store_v_transposed(v_ref, vt_ref, tk):
    pad = VT_ROWS - HEAD_DIM
    ones_row = (lax.broadcasted_iota(jnp.int32, (pad, tk), 0) == 0).astype(vt_ref.dtype)
    for j in range(vt_ref.shape[0]):
        vt_ref[j, 0:HEAD_DIM, :] = v_ref[j * tk:(j + 1) * tk, :].T
        vt_ref[j, HEAD_DIM:VT_ROWS, :] = ones_row


def _normalized(acc_ref):
    return acc_ref[0:HEAD_DIM, :] / acc_ref[HEAD_DIM:HEAD_DIM + 1, :]


def _diff_attn_kernel(lq1_ref, lk1_ref, lq2_ref, lk2_ref, g_ref, q_ref, k_ref, v_ref, o_ref,
                      qs_ref, vt_ref, s_ref, m_ref, acc_ref, *, tq, tk, lam_init):
    seq = q_ref.shape[0]
    _store_v_transposed(v_ref, vt_ref, tk)
    lam = (jnp.exp(jnp.sum(lq1_ref[...] * lk1_ref[...])) - jnp.exp(jnp.sum(lq2_ref[...] * lk2_ref[...]))
           + lam_init)

    for i in range(seq // tq):
        par = i % 2
        qs, m_par, acc_par = qs_ref.at[par], m_ref.at[par], acc_ref.at[par]
        q = q_ref[i * tq:(i + 1) * tq, :]
        lane = lax.broadcasted_iota(jnp.int32, q.shape, 1)
        zero = jnp.zeros_like(q)
        qs[0:tq, :] = jnp.where(lane < DIFF_SUB, q, zero)
        qs[tq:2 * tq, :] = jnp.where(lane >= DIFF_SUB, q, zero)

        def scores(j, qs=qs):
            k = k_ref[pl.ds(pl.multiple_of(j * tk, tk), tk), :]
            return lax.dot_general(k, qs[...], _NT, preferred_element_type=F32)

        j_diag = (i * tq) // tk

        def diag_scores(i=i, j_diag=j_diag, scores=scores):
            s_t = scores(j_diag)
            kpos = j_diag * tk + lax.broadcasted_iota(jnp.int32, s_t.shape, 0)
            qpos = i * tq + (lax.broadcasted_iota(jnp.int32, s_t.shape, 1) & (tq - 1))
            return jnp.where(kpos <= qpos, s_t, NEG_INF)

        _flash_pipeline(j_diag, j_diag, diag_scores, scores, vt_ref, s_ref, (2 * par, 2 * par + 1),
                        m_par, acc_par)

        o_t = _normalized(acc_par)
        o_t = o_t[:, 0:tq] - lam * o_t[:, tq:2 * tq]
        o_t = o_t * lax.rsqrt(jnp.mean(jnp.square(o_t), axis=0, keepdims=True) + RMS_EPS)
        o_t = o_t * g_ref[...] * (1.0 - lam_init)
        o_ref[i * tq:(i + 1) * tq, :] = o_t.T.astype(o_ref.dtype)


def _diff_attn(proj, lam_params, subln_g, batch, seq, n_heads, q_col, k_col, v_col, lam_init):
    tq = min(512, seq)
    tk = min(512, seq)
    assert seq % tk == 0 and tk % tq == 0 and (tq & (tq - 1)) == 0
    small = lambda n: pl.BlockSpec((1, n), lambda b, h: (0, 0))
    head = lambda col: pl.BlockSpec((seq, HEAD_DIM), lambda b, h: (b, col + h))
    kern = functools.partial(_diff_attn_kernel, tq=tq, tk=tk, lam_init=lam_init)
    return pl.pallas_call(
        kern,
        out_shape=jax.ShapeDtypeStruct((batch * seq, n_heads * HEAD_DIM), BF16),
        grid=(batch, n_heads),
        in_specs=[small(DIFF_SUB)] * 4 + [
            pl.BlockSpec((HEAD_DIM, 1), lambda b, h: (0, 0)),
            head(q_col), head(k_col), head(v_col),
        ],
        out_specs=head(0),
        scratch_shapes=[
            pltpu.VMEM((2, 2 * tq, HEAD_DIM), BF16),
            pltpu.VMEM((seq // tk, VT_ROWS, tk), BF16),
            pltpu.VMEM((4, tk, 2 * tq), F32),
            pltpu.VMEM((2, 1, 2 * tq), F32),
            pltpu.VMEM((2, VT_ROWS, 2 * tq), F32),
        ],
        compiler_params=_cparams(2, 40),
        name="diff_attn",
    )(*lam_params, subln_g, proj, proj, proj)


def _moba_attn_kernel(q_ref, k_ref, v_ref, o_ref, kmh_ref, kml_ref, vt_ref, sel_ref, s_ref, m_ref, acc_ref,
                      *, n_blocks, t):
    seq = q_ref.shape[0]
    blk = MOBA_BLOCK
    per = t // blk
    nbp = kmh_ref.shape[0]

    _store_v_transposed(v_ref, vt_ref, t)
    kmean = jnp.sum(k_ref[...].astype(F32).reshape(n_blocks, blk, HEAD_DIM), axis=1) * (1.0 / blk)
    if n_blocks < nbp:
        kmean = jnp.concatenate([kmean, jnp.zeros((nbp - n_blocks, HEAD_DIM), F32)], axis=0)
    hi = kmean.astype(BF16)
    kmh_ref[...] = hi
    kml_ref[...] = (kmean - hi.astype(F32)).astype(BF16)

    q_all = q_ref[...]
    gate = (lax.dot_general(kmh_ref[...], q_all, _NT, preferred_element_type=F32)
            + lax.dot_general(kml_ref[...], q_all, _NT, preferred_element_type=F32))
    blk_id = lax.broadcasted_iota(jnp.int32, (nbp, seq), 0)
    q_blk = lax.broadcasted_iota(jnp.int32, (nbp, seq), 1) // blk
    blk_f = blk_id.astype(F32)
    gate = jnp.where(blk_id < q_blk, gate, NEG_INF)
    sel = jnp.zeros((nbp, seq), F32)
    for _ in range(min(MOBA_TOPK, n_blocks)):
        mx = jnp.max(gate, axis=0, keepdims=True)
        cand = jnp.where(jnp.logical_and(gate == mx, gate > NEG_INF), blk_f, float(nbp))
        pick = blk_f == jnp.min(cand, axis=0, keepdims=True)
        sel = jnp.where(pick, 1.0, sel)
        gate = jnp.where(pick, NEG_INF, gate)
    sel_ref[...] = sel

    for i in range(seq // t):
        par = i % 2
        m_par, acc_par = m_ref.at[par], acc_ref.at[par]
        cols = slice(i * t, (i + 1) * t)

        def scores(g, cols=cols):
            k = k_ref[pl.ds(pl.multiple_of(g * t, t), t), :]
            return lax.dot_general(k, q_ref[cols, :], _NT, preferred_element_type=F32)

        def picked(g, cols=cols):
            rows = [jnp.broadcast_to(sel_ref[pl.ds(g * per + c, 1), cols] > 0.5, (blk, t))
                    for c in range(per)]
            return rows[0] if per == 1 else jnp.concatenate(rows, axis=0)

        def past_scores(g, scores=scores, picked=picked):
            return jnp.where(picked(g), scores(g), NEG_INF)

        def diag_scores(i=i, scores=scores, picked=picked):
            r = lax.broadcasted_iota(jnp.int32, (t, t), 0)
            c = lax.broadcasted_iota(jnp.int32, (t, t), 1)
            own_causal = jnp.logical_and(r // blk == c // blk, r <= c)
            return jnp.where(jnp.logical_or(own_causal, picked(i)), scores(i), NEG_INF)

        _flash_pipeline(i, i, diag_scores, past_scores, vt_ref, s_ref, (2 * par, 2 * par + 1),
                        m_par, acc_par)
        o_ref[cols, :] = _normalized(acc_par).T.astype(o_ref.dtype)


def _moba_attn(proj, batch, seq, n_heads, q_col, k_col, v_col):
    t = min(2 * MOBA_BLOCK, seq)
    assert seq % t == 0 and t % MOBA_BLOCK == 0
    nb = seq // MOBA_BLOCK
    nbp = -(-nb // BF16_SUBLANES) * BF16_SUBLANES
    nt = seq // t
    head = lambda col: pl.BlockSpec((seq, HEAD_DIM), lambda b, h: (b, col + h))
    kern = functools.partial(_moba_attn_kernel, n_blocks=nb, t=t)
    return pl.pallas_call(
        kern,
        out_shape=jax.ShapeDtypeStruct((batch * seq, n_heads * HEAD_DIM), BF16),
        grid=(batch, n_heads),
        in_specs=[head(q_col), head(k_col), head(v_col)],
        out_specs=head(0),
        scratch_shapes=[
            pltpu.VMEM((nbp, HEAD_DIM), BF16),
            pltpu.VMEM((nbp, HEAD_DIM), BF16),
            pltpu.VMEM((nt, VT_ROWS, t), BF16),
            pltpu.VMEM((nbp, seq), F32),
            pltpu.VMEM((4, t, t), F32),
            pltpu.VMEM((2, 1, t), F32),
            pltpu.VMEM((2, VT_ROWS, t), F32),
        ],
        compiler_params=_cparams(2, 40),
        name="moba_attn",
    )(proj, proj, proj)


def _layer_norm(y, g, b):
    mu = jnp.mean(y, axis=-1, keepdims=True)
    yc = y - mu
    var = jnp.mean(jnp.square(yc), axis=-1, keepdims=True)
    return yc * lax.rsqrt(var + LN_EPS) * g + b


def _out_ln1_kernel(a_ref, b_ref, wa_ref, wb_ref, x_ref, g_ref, beta_ref, h_ref, hb_ref):
    for ch in range(x_ref.shape[0] // ROW_CHUNK):
        rows = slice(ch * ROW_CHUNK, (ch + 1) * ROW_CHUNK)
        mix = (jnp.dot(a_ref[rows, :], wa_ref[...], preferred_element_type=F32)
               + jnp.dot(b_ref[rows, :], wb_ref[...], preferred_element_type=F32))
        h = _layer_norm(DEEPNORM_ALPHA * x_ref[rows, :] + mix, g_ref[...], beta_ref[...])
        h_ref[rows, :] = h
        hb_ref[rows, :] = h.astype(hb_ref.dtype)


def _out_ln1(a_out, b_out, w_out, x2, g, beta):
    m, d = x2.shape
    wa, wb = a_out.shape[1], b_out.shape[1]
    bm = 512
    assert m % bm == 0
    row = lambda w: pl.BlockSpec((bm, w), lambda i: (i, 0))
    vec = pl.BlockSpec((1, d), lambda i: (0, 0))
    return pl.pallas_call(
        _out_ln1_kernel,
        out_shape=(jax.ShapeDtypeStruct((m, d), F32), jax.ShapeDtypeStruct((m, d), BF16)),
        grid=(m // bm,),
        in_specs=[
            row(wa), row(wb),
            pl.BlockSpec((wa, d), lambda i: (0, 0)),
            pl.BlockSpec((wb, d), lambda i: (wa // wb, 0)),
            row(d), vec, vec,
        ],
        out_specs=(row(d), row(d)),
        compiler_params=_cparams(1, 56),
        name="out_ln1",
    )(a_out, b_out, w_out, w_out, x2, g, beta)


def _ffn_up_kernel(h_ref, wg32_ref, wv32_ref, cw_ref, cb_ref, o_ref, wg_ref, wv_ref, gbuf_ref,
                   *, bm, tiles_per_seq):
    r = pl.program_id(1)

    @pl.when(r == 0)
    def _():
        wg_ref[...] = wg32_ref[...].astype(BF16)
        wv_ref[...] = wv32_ref[...].astype(BF16)

    @pl.when(r % tiles_per_seq == 0)
    def _():
        gbuf_ref[0:CONV_HALO, :] = jnp.zeros((CONV_HALO, gbuf_ref.shape[1]), F32)

    cw = cw_ref[...]
    for c in range(bm // ROW_CHUNK):
        lo = c * ROW_CHUNK
        h = h_ref[lo:lo + ROW_CHUNK, :]
        g = jnp.dot(h, wg_ref[...], preferred_element_type=F32)
        val = jnp.dot(h, wv_ref[...], preferred_element_type=F32)
        gbuf_ref[CONV_HALO + lo:CONV_HALO + lo + ROW_CHUNK, :] = g
        g1 = gbuf_ref[pl.ds(CONV_HALO + lo - 1, ROW_CHUNK), :]
        g2 = gbuf_ref[pl.ds(CONV_HALO + lo - 2, ROW_CHUNK), :]
        gc = cb_ref[...] + (cw[0:1, :] * g2 + cw[1:2, :] * g1 + cw[2:3, :] * g)
        act = gc * jax.nn.sigmoid(gc) * val
        o_ref[lo:lo + ROW_CHUNK, :] = act.astype(o_ref.dtype)
    gbuf_ref[0:CONV_HALO, :] = gbuf_ref[bm:bm + CONV_HALO, :]


def _ffn_up(hb, w_up, conv_w, conv_b, seq):
    m, d = hb.shape
    f = conv_w.shape[1]
    bm = min(2048, seq)
    bn = 512
    assert seq % bm == 0 and f % bn == 0
    nct = f // bn
    n_r = m // bm
    kern = functools.partial(_ffn_up_kernel, bm=bm, tiles_per_seq=seq // bm)

    def w_col(c, r, adv):
        return jnp.minimum(c + (r >= adv).astype(jnp.int32), nct - 1)

    adv_g, adv_v = max(1, n_r // 3), max(1, 2 * n_r // 3)
    return pl.pallas_call(
        kern,
        out_shape=jax.ShapeDtypeStruct((m, f), BF16),
        grid=(nct, n_r),
        in_specs=[
            pl.BlockSpec((bm, d), lambda c, r: (r, 0)),
            pl.BlockSpec((d, bn), lambda c, r: (0, w_col(c, r, adv_g))),
            pl.BlockSpec((d, bn), lambda c, r: (0, nct + w_col(c, r, adv_v))),
            pl.BlockSpec((CONV_W, bn), lambda c, r: (0, c)),
            pl.BlockSpec((1, bn), lambda c, r: (0, c)),
        ],
        out_specs=pl.BlockSpec((bm, bn), lambda c, r: (r, c)),
        scratch_shapes=[pltpu.VMEM((d, bn), BF16), pltpu.VMEM((d, bn), BF16),
                        pltpu.VMEM((bm + CONV_HALO, bn), F32)],
        compiler_params=_cparams(2, 54),
        name="ffn_up",
    )(hb, w_up, w_up, conv_w, conv_b)


def _ffn_down_kernel(a_ref, w_ref, *refs):
    *h_slabs, g_ref, beta_ref, o_ref = refs
    k = pl.program_id(1)
    last = pl.num_programs(1) - 1

    def chunks(step):
        for ch in range(o_ref.shape[0] // ROW_CHUNK):
            rows = slice(ch * ROW_CHUNK, (ch + 1) * ROW_CHUNK)
            step(rows, jnp.dot(a_ref[rows, :], w_ref[...], preferred_element_type=F32))

    def first(rows, part):
        wid = h_slabs[0].shape[1]
        for s, h_ref in enumerate(h_slabs):
            cols = slice(s * wid, (s + 1) * wid)
            o_ref[rows, cols] = DEEPNORM_ALPHA * h_ref[rows, :] + part[:, cols]

    def middle(rows, part):
        o_ref[rows, :] += part

    def final(rows, part):
        o_ref[rows, :] = _layer_norm(o_ref[rows, :] + part, g_ref[...], beta_ref[...])

    pl.when(k == 0)(lambda: chunks(first))
    pl.when(jnp.logical_and(k > 0, k < last))(lambda: chunks(middle))
    pl.when(k == last)(lambda: chunks(final))


def _ffn_down(act, w_down, h1, g, beta):
    m, f = act.shape
    d = w_down.shape[1]
    bm, bk = 1024, 512
    n_i, n_k = m // bm, f // bk
    assert m % bm == 0 and f % bk == 0 and n_k >= 2
    vec = pl.BlockSpec((1, d), lambda i, k: (0, 0))
    n_slab = 4 if d % (4 * LANES) == 0 else 1
    adv = [max(1, (s + 1) * n_k // (n_slab + 1)) for s in range(n_slab)]

    def slab_spec(s):
        return pl.BlockSpec((bm, d // n_slab),
                            lambda i, k: (jnp.minimum(i + (k >= adv[s]).astype(jnp.int32), n_i - 1), s))

    return pl.pallas_call(
        _ffn_down_kernel,
        out_shape=jax.ShapeDtypeStruct((m, d), F32),
        grid=(n_i, n_k),
        in_specs=[
            pl.BlockSpec((bm, bk), lambda i, k: (i, k)),
            pl.BlockSpec((bk, d), lambda i, k: (k, 0)),
        ] + [slab_spec(s) for s in range(n_slab)] + [vec, vec],
        out_specs=pl.BlockSpec((bm, d), lambda i, k: (i, 0)),
        compiler_params=_cparams(2, 48),
        name="ffn_down",
    )(act, w_down, *([h1] * n_slab), g, beta)


def kernel(x, w_in, lambda_q1, lambda_k1, lambda_q2, lambda_k2, subln_g, w_out, ln1_g, ln1_b, w_up,
           conv_w, conv_b, w_down, ln2_g, ln2_b):
    batch, seq, d_model = x.shape
    depth = w_in.shape[0]
    mix_w = w_out.shape[1]
    diff_w = moba_w = mix_w // 2
    n_dh, n_mh = diff_w // HEAD_DIM, moba_w // HEAD_DIM
    assert w_in.shape[2] == 3 * diff_w + 3 * moba_w

    h = x.reshape(batch * seq, d_model)
    for l in range(depth):
        lam_init = 0.8 - 0.6 * math.exp(-0.3 * l)
        proj = _proj_rope(h, w_in[l].astype(BF16), seq, diff_w, moba_w)
        lam_params = [p[l].reshape(1, DIFF_SUB) for p in (lambda_q1, lambda_k1, lambda_q2, lambda_k2)]
        a_out = _diff_attn(proj, lam_params, subln_g[l].reshape(HEAD_DIM, 1), batch, seq, n_dh,
                           0, n_dh, 2 * n_dh, lam_init)
        b_out = _moba_attn(proj, batch, seq, n_mh, 3 * n_dh, 3 * n_dh + n_mh, 3 * n_dh + 2 * n_mh)
        h1, h1b = _out_ln1(a_out, b_out, w_out[l].astype(BF16), h,
                           ln1_g[l].reshape(1, d_model), ln1_b[l].reshape(1, d_model))
        act = _ffn_up(h1b, w_up[l], conv_w[l], conv_b[l].reshape(1, -1), seq)
        h = _ffn_down(act, w_down[l].astype(BF16), h1, ln2_g[l].reshape(1, d_model),
                      ln2_b[l].reshape(1, d_model))
    return h.reshape(batch, seq, d_model)
```

```python
import functools
import math

import jax
import jax.numpy as jnp
from jax import lax
from jax.experimental import pallas as pl
from jax.experimental.pallas import tpu as pltpu

F32 = jnp.float32
BF16 = jnp.bfloat16

LANES = 128
BF16_SUBLANES = 16
HEAD_DIM = 128
VT_ROWS = HEAD_DIM + BF16_SUBLANES
LOG2E = 1.4426950408889634
DIFF_SUB = HEAD_DIM // 2
ROPE_THETA = 500000.0
ROT_FRACTION = 4
MOBA_BLOCK = 256
MOBA_TOPK = 3
CONV_W = 3
CONV_HALO = 8
ROW_CHUNK = 128
ACC_ROW_CHUNK = 256
LN_EPS = 1e-5
RMS_EPS = 1e-5
DEPTH = 1
DEEPNORM_ALPHA = (2.0 * DEPTH) ** 0.25
DIFF_SCALE = DIFF_SUB ** -0.5
MOBA_SCALE = HEAD_DIM ** -0.5
NEG_INF = float("-inf")

_NT = (((1,), (1,)), ((), ()))


def _cparams(n_axes, vmem_mib):
    return pltpu.CompilerParams(
        dimension_semantics=("arbitrary",) * n_axes,
        vmem_limit_bytes=vmem_mib * 1024 * 1024,
    )


def _rope_coeffs(seq, period):
    rot = period // ROT_FRACTION
    half = rot // 2
    inv = 1.0 / (ROPE_THETA ** (jnp.arange(0, rot, 2, dtype=F32) / rot))
    dim = [l % period for l in range(LANES)]
    lane_inv = inv[jnp.asarray([d % half for d in dim])]
    first = jnp.asarray([d < half for d in dim])[None, :]
    second = jnp.asarray([half <= d < rot for d in dim])[None, :]
    ang = jnp.arange(seq, dtype=F32)[:, None] * lane_inv[None, :]
    cos, sin = jnp.cos(ang), jnp.sin(ang)
    c = jnp.where(jnp.logical_or(first, second), cos, 1.0)
    s1 = jnp.where(first, -sin, 0.0)
    s2 = jnp.where(second, sin, 0.0)
    return jnp.stack([c, s1, s2], axis=0)


def _proj_rope_kernel(x_ref, w_ref, tab_ref, o_ref, xb_ref, *, diff_tiles, moba_tiles, q_scales,
                      n_heads_blk):
    j = pl.program_id(1)
    scale = jnp.float32(1.0)
    for t, s in q_scales:
        scale = jnp.where(j == t, jnp.float32(s), scale)

    def tile(half, first):
        for ch in range(x_ref.shape[0] // ROW_CHUNK):
            rows = slice(ch * ROW_CHUNK, (ch + 1) * ROW_CHUNK)
            if first:
                xb_ref[rows, :] = x_ref[rows, :].astype(BF16)
            acc = jnp.dot(xb_ref[rows, :], w_ref[...], preferred_element_type=F32)
            if half is None:
                o_ref[rows, :] = acc.astype(o_ref.dtype)
                continue
            c, s1, s2 = tab_ref[0, 0, rows, :], tab_ref[0, 1, rows, :], tab_ref[0, 2, rows, :]
            for hb in range(n_heads_blk):
                a = acc[:, hb * LANES:(hb + 1) * LANES]
                o = a * c + pltpu.roll(a, LANES - half, 1) * s1 + pltpu.roll(a, half, 1) * s2
                o_ref[rows, hb * LANES:(hb + 1) * LANES] = (o * scale).astype(o_ref.dtype)

    assert diff_tiles[0] == 0
    is_diff = functools.reduce(jnp.logical_or, [j == t for t in diff_tiles[1:]])
    is_moba = functools.reduce(jnp.logical_or, [j == t for t in moba_tiles])
    diff_half = DIFF_SUB // ROT_FRACTION // 2
    moba_half = HEAD_DIM // ROT_FRACTION // 2

    @pl.when(j == 0)
    def _():
        tile(diff_half, True)

    @pl.when(is_diff)
    def _():
        tile(diff_half, False)

    @pl.when(is_moba)
    def _():
        tile(moba_half, False)

    @pl.when(jnp.logical_not(jnp.logical_or(j == 0, jnp.logical_or(is_diff, is_moba))))
    def _():
        tile(None, False)


def _proj_rope(x2, w_in, seq, diff_w, moba_w):
    m, d = x2.shape
    n = w_in.shape[1]
    bm = min(1024, seq)
    bn = 1024
    assert seq % bm == 0 and m % bm == 0 and diff_w % bn == 0 and moba_w % bn == 0
    dt, mt = diff_w // bn, moba_w // bn
    q_d_tiles = list(range(0, dt))
    k_d_tiles = list(range(dt, 2 * dt))
    q_m_tiles = list(range(3 * dt, 3 * dt + mt))
    k_m_tiles = list(range(3 * dt + mt, 3 * dt + 2 * mt))
    tabs = jnp.stack([_rope_coeffs(seq, DIFF_SUB), _rope_coeffs(seq, HEAD_DIM)], axis=0)

    def kind(j):
        k = jnp.int32(0)
        for t in q_m_tiles + k_m_tiles:
            k = jnp.where(j == t, 1, k)
        return k

    spt = seq // bm
    q_scales = ([(t, DIFF_SCALE * LOG2E) for t in q_d_tiles] + [(t, MOBA_SCALE * LOG2E) for t in q_m_tiles])
    kern = functools.partial(_proj_rope_kernel, diff_tiles=q_d_tiles + k_d_tiles,
                             moba_tiles=q_m_tiles + k_m_tiles, q_scales=q_scales, n_heads_blk=bn // LANES)
    return pl.pallas_call(
        kern,
        out_shape=jax.ShapeDtypeStruct((m, n), BF16),
        grid=(m // bm, n // bn),
        in_specs=[
            pl.BlockSpec((bm, d), lambda i, j: (i, 0)),
            pl.BlockSpec((d, bn), lambda i, j: (0, j)),
            pl.BlockSpec((1, 3, bm, LANES), lambda i, j: (kind(j), 0, i % spt, 0)),
        ],
        out_specs=pl.BlockSpec((bm, bn), lambda i, j: (i, j)),
        scratch_shapes=[pltpu.VMEM((bm, d), BF16)],
        compiler_params=_cparams(2, 56),
        name="proj_rope",
    )(x2, w_in, tabs)


def _softmax_step(s_t, v_t, m_ref, acc_ref):
    m_prev = m_ref[...]
    m_new = jnp.maximum(m_prev, jnp.max(s_t, axis=0, keepdims=True))
    alpha = jnp.exp2(m_prev - m_new)
    p = jnp.exp2(s_t - m_new)
    acc_ref[...] = acc_ref[...] * alpha + jnp.dot(v_t, p.astype(v_t.dtype), preferred_element_type=F32)
    m_ref[...] = m_new


def _flash_pipeline(n_full, j_diag, diag_scores, full_scores, vt_ref, s_ref, slots, m_ref, acc_ref):
    sa, sb = slots
    m_ref[...] = jnp.full(m_ref.shape, NEG_INF, F32)
    acc_ref[...] = jnp.zeros(acc_ref.shape, F32)
    s_ref[sa] = diag_scores()
    n_pairs = (n_full + 1) // 2

    def pair(p, carry):
        s_ref[sb] = full_scores(2 * p)
        _softmax_step(s_ref[sa], vt_ref[jnp.where(p == 0, j_diag, 2 * p - 1)], m_ref, acc_ref)
        s_ref[sa] = full_scores(jnp.minimum(2 * p + 1, max(n_full - 1, 0)))
        _softmax_step(s_ref[sb], vt_ref[2 * p], m_ref, acc_ref)
        return carry

    def two_pairs(pp, carry):
        return pair(2 * pp + 1, pair(2 * pp, carry))

    if n_pairs >= 2:
        lax.fori_loop(0, n_pairs // 2, two_pairs, 0)
    if n_pairs % 2 == 1:
        pair(jnp.int32(n_pairs - 1), 0)
    if (n_full + 1) % 2 == 1:
        _softmax_step(s_ref[sa], vt_ref[j_diag if n_full == 0 else n_full - 1], m_ref, acc_ref)


def _store_v_transposed(v_ref, vt_ref, tk):
    pad = VT_ROWS - HEAD_DIM
    ones_row = (lax.broadcasted_iota(jnp.int32, (pad, tk), 0) == 0).astype(vt_ref.dtype)
    for j in range(vt_ref.shape[0]):
        vt_ref[j, 0:HEAD_DIM, :] = v_ref[j * tk:(j + 1) * tk, :].T
        vt_ref[j, HEAD_DIM:VT_ROWS, :] = ones_row


def _normalized(acc_ref):
    return acc_ref[0:HEAD_DIM, :] / acc_ref[HEAD_DIM:HEAD_DIM + 1, :]


def _diff_attn_kernel(lq1_ref, lk1_ref, lq2_ref, lk2_ref, g_ref, q_ref, k_ref, v_ref, o_ref,
                      qs_ref, vt_ref, s_ref, m_ref, acc_ref, *, tq, tk, lam_init):
    seq = q_ref.shape[0]
    _store_v_transposed(v_ref, vt_ref, tk)
    lam = (jnp.exp(jnp.sum(lq1_ref[...] * lk1_ref[...])) - jnp.exp(jnp.sum(lq2_ref[...] * lk2_ref[...]))
           + lam_init)

    for i in range(seq // tq):
        par = i % 2
        qs, m_par, acc_par = qs_ref.at[par], m_ref.at[par], acc_ref.at[par]
        q = q_ref[i * tq:(i + 1) * tq, :]
        lane = lax.broadcasted_iota(jnp.int32, q.shape, 1)
        zero = jnp.zeros_like(q)
        qs[0:tq, :] = jnp.where(lane < DIFF_SUB, q, zero)
        qs[tq:2 * tq, :] = jnp.where(lane >= DIFF_SUB, q, zero)

        def scores(j, qs=qs):
            k = k_ref[pl.ds(pl.multiple_of(j * tk, tk), tk), :]
            return lax.dot_general(k, qs[...], _NT, preferred_element_type=F32)

        j_diag = (i * tq) // tk

        def diag_scores(i=i, j_diag=j_diag, scores=scores, qs=qs):
            if tq != tk:
                s_t = scores(j_diag)
                kpos = j_diag * tk + lax.broadcasted_iota(jnp.int32, s_t.shape, 0)
                qpos = i * tq + (lax.broadcasted_iota(jnp.int32, s_t.shape, 1) & (tq - 1))
                return jnp.where(kpos <= qpos, s_t, NEG_INF)
            hq = tq // 2
            k = k_ref[j_diag * tk:(j_diag + 1) * tk, :]
            top = lax.dot_general(k[0:hq, :], qs[...], _NT, preferred_element_type=F32)
            r = lax.broadcasted_iota(jnp.int32, top.shape, 0)
            c = lax.broadcasted_iota(jnp.int32, top.shape, 1) & (tq - 1)
            top = jnp.where(r <= c, top, NEG_INF)
            q_late = jnp.concatenate([qs[hq:tq, :], qs[tq + hq:2 * tq, :]], axis=0)
            bot = lax.dot_general(k[hq:tq, :], q_late, _NT, preferred_element_type=F32)
            r = lax.broadcasted_iota(jnp.int32, bot.shape, 0)
            c = lax.broadcasted_iota(jnp.int32, bot.shape, 1) & (hq - 1)
            bot = jnp.where(r <= c, bot, NEG_INF)
            neg = jnp.full((hq, hq), NEG_INF, F32)
            bot = jnp.concatenate([neg, bot[:, 0:hq], neg, bot[:, hq:tq]], axis=1)
            return jnp.concatenate([top, bot], axis=0)

        _flash_pipeline(j_diag, j_diag, diag_scores, scores, vt_ref, s_ref, (2 * par, 2 * par + 1),
                        m_par, acc_par)

        o_t = _normalized(acc_par)
        o_t = o_t[:, 0:tq] - lam * o_t[:, tq:2 * tq]
        o_t = o_t * lax.rsqrt(jnp.mean(jnp.square(o_t), axis=0, keepdims=True) + RMS_EPS)
        o_t = o_t * g_ref[...] * (1.0 - lam_init)
        o_ref[i * tq:(i + 1) * tq, :] = o_t.T.astype(o_ref.dtype)


def _diff_attn(proj, lam_params, subln_g, batch, seq, n_heads, q_col, k_col, v_col, lam_init):
    tq = min(512, seq)
    tk = min(512, seq)
    assert seq % tk == 0 and tk % tq == 0 and (tq & (tq - 1)) == 0
    small = lambda n: pl.BlockSpec((1, n), lambda b, h: (0, 0))
    head = lambda col: pl.BlockSpec((seq, HEAD_DIM), lambda b, h: (b, col + h))
    kern = functools.partial(_diff_attn_kernel, tq=tq, tk=tk, lam_init=lam_init)
    return pl.pallas_call(
        kern,
        out_shape=jax.ShapeDtypeStruct((batch * seq, n_heads * HEAD_DIM), BF16),
        grid=(batch, n_heads),
        in_specs=[small(DIFF_SUB)] * 4 + [
            pl.BlockSpec((HEAD_DIM, 1), lambda b, h: (0, 0)),
            head(q_col), head(k_col), head(v_col),
        ],
        out_specs=head(0),
        scratch_shapes=[
            pltpu.VMEM((2, 2 * tq, HEAD_DIM), BF16),
            pltpu.VMEM((seq // tk, VT_ROWS, tk), BF16),
            pltpu.VMEM((4, tk, 2 * tq), F32),
            pltpu.VMEM((2, 1, 2 * tq), F32),
            pltpu.VMEM((2, VT_ROWS, 2 * tq), F32),
        ],
        compiler_params=_cparams(2, 40),
        name="diff_attn",
    )(*lam_params, subln_g, proj, proj, proj)


def _moba_attn_kernel(q_ref, k_ref, v_ref, o_ref, kmh_ref, kml_ref, vt_ref, sel_ref, s_ref, m_ref, acc_ref,
                      *, n_blocks, t):
    seq = q_ref.shape[0]
    blk = MOBA_BLOCK
    per = t // blk
    nbp = kmh_ref.shape[0]

    _store_v_transposed(v_ref, vt_ref, t)
    kmean = jnp.sum(k_ref[...].astype(F32).reshape(n_blocks, blk, HEAD_DIM), axis=1) * (1.0 / blk)
    if n_blocks < nbp:
        kmean = jnp.concatenate([kmean, jnp.zeros((nbp - n_blocks, HEAD_DIM), F32)], axis=0)
    hi = kmean.astype(BF16)
    kmh_ref[...] = hi
    kml_ref[...] = (kmean - hi.astype(F32)).astype(BF16)

    q_all = q_ref[...]
    gate = (lax.dot_general(kmh_ref[...], q_all, _NT, preferred_element_type=F32)
            + lax.dot_general(kml_ref[...], q_all, _NT, preferred_element_type=F32))
    blk_id = lax.broadcasted_iota(jnp.int32, (nbp, seq), 0)
    q_blk = lax.broadcasted_iota(jnp.int32, (nbp, seq), 1) // blk
    blk_f = blk_id.astype(F32)
    gate = jnp.where(blk_id < q_blk, gate, NEG_INF)
    sel = jnp.zeros((nbp, seq), F32)
    for _ in range(min(MOBA_TOPK, n_blocks)):
        mx = jnp.max(gate, axis=0, keepdims=True)
        cand = jnp.where(jnp.logical_and(gate == mx, gate > NEG_INF), blk_f, float(nbp))
        pick = blk_f == jnp.min(cand, axis=0, keepdims=True)
        sel = jnp.where(pick, 1.0, sel)
        gate = jnp.where(pick, NEG_INF, gate)
    sel_ref[...] = sel

    for i in range(seq // t):
        par = i % 2
        m_par, acc_par = m_ref.at[par], acc_ref.at[par]
        cols = slice(i * t, (i + 1) * t)

        def scores(g, cols=cols):
            k = k_ref[pl.ds(pl.multiple_of(g * t, t), t), :]
            return lax.dot_general(k, q_ref[cols, :], _NT, preferred_element_type=F32)

        def picked(g, cols=cols):
            rows = [jnp.broadcast_to(sel_ref[pl.ds(g * per + c, 1), cols] > 0.5, (blk, t))
                    for c in range(per)]
            return rows[0] if per == 1 else jnp.concatenate(rows, axis=0)

        def past_scores(g, scores=scores, picked=picked):
            return jnp.where(picked(g), scores(g), NEG_INF)

        def diag_scores(i=i, cols=cols, scores=scores, picked=picked):
            if per != 2:
                r = lax.broadcasted_iota(jnp.int32, (t, t), 0)
                c = lax.broadcasted_iota(jnp.int32, (t, t), 1)
                own_causal = jnp.logical_and(r // blk == c // blk, r <= c)
                return jnp.where(jnp.logical_or(own_causal, picked(i)), scores(i), NEG_INF)
            k = k_ref[i * t:(i + 1) * t, :]
            q = q_ref[cols, :]
            top = lax.dot_general(k[0:blk, :], q, _NT, preferred_element_type=F32)
            r = lax.broadcasted_iota(jnp.int32, top.shape, 0)
            c = lax.broadcasted_iota(jnp.int32, top.shape, 1)
            own_causal = jnp.logical_and(c < blk, r <= c)
            seen = jnp.logical_or(own_causal, jnp.broadcast_to(sel_ref[pl.ds(i * per, 1), cols] > 0.5, top.shape))
            top = jnp.where(seen, top, NEG_INF)
            bot = lax.dot_general(k[blk:t, :], q[blk:t, :], _NT, preferred_element_type=F32)
            r = lax.broadcasted_iota(jnp.int32, bot.shape, 0)
            c = lax.broadcasted_iota(jnp.int32, bot.shape, 1)
            bot = jnp.where(r <= c, bot, NEG_INF)
            bot = jnp.concatenate([jnp.full((blk, blk), NEG_INF, F32), bot], axis=1)
            return jnp.concatenate([top, bot], axis=0)

        _flash_pipeline(i, i, diag_scores, past_scores, vt_ref, s_ref, (2 * par, 2 * par + 1),
                        m_par, acc_par)
        o_ref[cols, :] = _normalized(acc_par).T.astype(o_ref.dtype)


def _moba_attn(proj, batch, seq, n_heads, q_col, k_col, v_col):
    t = min(2 * MOBA_BLOCK, seq)
    assert seq % t == 0 and t % MOBA_BLOCK == 0
    nb = seq // MOBA_BLOCK
    nbp = -(-nb // BF16_SUBLANES) * BF16_SUBLANES
    nt = seq // t
    head = lambda col: pl.BlockSpec((seq, HEAD_DIM), lambda b, h: (b, col + h))
    kern = functools.partial(_moba_attn_kernel, n_blocks=nb, t=t)
    return pl.pallas_call(
        kern,
        out_shape=jax.ShapeDtypeStruct((batch * seq, n_heads * HEAD_DIM), BF16),
        grid=(batch, n_heads),
        in_specs=[head(q_col), head(k_col), head(v_col)],
        out_specs=head(0),
        scratch_shapes=[
            pltpu.VMEM((nbp, HEAD_DIM), BF16),
            pltpu.VMEM((nbp, HEAD_DIM), BF16),
            pltpu.VMEM((nt, VT_ROWS, t), BF16),
            pltpu.VMEM((nbp, seq), F32),
            pltpu.VMEM((4, t, t), F32),
            pltpu.VMEM((2, 1, t), F32),
            pltpu.VMEM((2, VT_ROWS, t), F32),
        ],
        compiler_params=_cparams(2, 40),
        name="moba_attn",
    )(proj, proj, proj)


def _layer_norm(y, g, b):
    mu = jnp.mean(y, axis=-1, keepdims=True)
    yc = y - mu
    var = jnp.mean(jnp.square(yc), axis=-1, keepdims=True)
    return yc * lax.rsqrt(var + LN_EPS) * g + b


def _out_ln1_kernel(a_ref, b_ref, wa_ref, wb_ref, x_ref, g_ref, beta_ref, h_ref, hb_ref):
    for ch in range(x_ref.shape[0] // ROW_CHUNK):
        rows = slice(ch * ROW_CHUNK, (ch + 1) * ROW_CHUNK)
        mix = (jnp.dot(a_ref[rows, :], wa_ref[...], preferred_element_type=F32)
               + jnp.dot(b_ref[rows, :], wb_ref[...], preferred_element_type=F32))
        h = _layer_norm(DEEPNORM_ALPHA * x_ref[rows, :] + mix, g_ref[...], beta_ref[...])
        h_ref[rows, :] = h
        hb_ref[rows, :] = h.astype(hb_ref.dtype)


def _out_ln1(a_out, b_out, w_out, x2, g, beta):
    m, d = x2.shape
    wa, wb = a_out.shape[1], b_out.shape[1]
    bm = 512
    assert m % bm == 0
    row = lambda w: pl.BlockSpec((bm, w), lambda i: (i, 0))
    vec = pl.BlockSpec((1, d), lambda i: (0, 0))
    return pl.pallas_call(
        _out_ln1_kernel,
        out_shape=(jax.ShapeDtypeStruct((m, d), F32), jax.ShapeDtypeStruct((m, d), BF16)),
        grid=(m // bm,),
        in_specs=[
            row(wa), row(wb),
            pl.BlockSpec((wa, d), lambda i: (0, 0)),
            pl.BlockSpec((wb, d), lambda i: (wa // wb, 0)),
            row(d), vec, vec,
        ],
        out_specs=(row(d), row(d)),
        compiler_params=_cparams(1, 56),
        name="out_ln1",
    )(a_out, b_out, w_out, w_out, x2, g, beta)


def _ffn_up_kernel(h_ref, wg32_ref, wv32_ref, cw_ref, cb_ref, o_ref, wg_ref, wv_ref, gbuf_ref,
                   *, bm, tiles_per_seq):
    r = pl.program_id(1)

    @pl.when(r == 0)
    def _():
        wg_ref[...] = wg32_ref[...].astype(BF16)
        wv_ref[...] = wv32_ref[...].astype(BF16)

    @pl.when(r % tiles_per_seq == 0)
    def _():
        gbuf_ref[0:CONV_HALO, :] = jnp.zeros((CONV_HALO, gbuf_ref.shape[1]), F32)

    cw = cw_ref[...]
    for c in range(bm // ROW_CHUNK):
        lo = c * ROW_CHUNK
        h = h_ref[lo:lo + ROW_CHUNK, :]
        g = jnp.dot(h, wg_ref[...], preferred_element_type=F32)
        val = jnp.dot(h, wv_ref[...], preferred_element_type=F32)
        gbuf_ref[CONV_HALO + lo:CONV_HALO + lo + ROW_CHUNK, :] = g
        g1 = gbuf_ref[pl.ds(CONV_HALO + lo - 1, ROW_CHUNK), :]
        g2 = gbuf_ref[pl.ds(CONV_HALO + lo - 2, ROW_CHUNK), :]
        gc = cb_ref[...] + (cw[0:1, :] * g2 + cw[1:2, :] * g1 + cw[2:3, :] * g)
        act = gc * jax.nn.sigmoid(gc) * val
        o_ref[lo:lo + ROW_CHUNK, :] = act.astype(o_ref.dtype)
    gbuf_ref[0:CONV_HALO, :] = gbuf_ref[bm:bm + CONV_HALO, :]


def _ffn_up(hb, w_up, conv_w, conv_b, seq):
    m, d = hb.shape
    f = conv_w.shape[1]
    bm = min(2048, seq)
    bn = 512
    assert seq % bm == 0 and f % bn == 0
    nct = f // bn
    kern = functools.partial(_ffn_up_kernel, bm=bm, tiles_per_seq=seq // bm)
    return pl.pallas_call(
        kern,
        out_shape=jax.ShapeDtypeStruct((m, f), BF16),
        grid=(nct, m // bm),
        in_specs=[
            pl.BlockSpec((bm, d), lambda c, r: (r, 0)),
            pl.BlockSpec((d, bn), lambda c, r: (0, c)),
            pl.BlockSpec((d, bn), lambda c, r: (0, nct + c)),
            pl.BlockSpec((CONV_W, bn), lambda c, r: (0, c)),
            pl.BlockSpec((1, bn), lambda c, r: (0, c)),
        ],
        out_specs=pl.BlockSpec((bm, bn), lambda c, r: (r, c)),
        scratch_shapes=[pltpu.VMEM((d, bn), BF16), pltpu.VMEM((d, bn), BF16),
                        pltpu.VMEM((bm + CONV_HALO, bn), F32)],
        compiler_params=_cparams(2, 54),
        name="ffn_up",
    )(hb, w_up, w_up, conv_w, conv_b)


def _ffn_down_kernel(a_ref, w_ref, h_ref, g_ref, beta_ref, o_ref):
    k = pl.program_id(1)
    last = pl.num_programs(1) - 1

    def chunks(step):
        for ch in range(o_ref.shape[0] // ACC_ROW_CHUNK):
            rows = slice(ch * ACC_ROW_CHUNK, (ch + 1) * ACC_ROW_CHUNK)
            step(rows, jnp.dot(a_ref[rows, :], w_ref[...], preferred_element_type=F32))

    def first(rows, part):
        o_ref[rows, :] = DEEPNORM_ALPHA * h_ref[rows, :] + part

    def middle(rows, part):
        o_ref[rows, :] += part

    def final(rows, part):
        o_ref[rows, :] = _layer_norm(o_ref[rows, :] + part, g_ref[...], beta_ref[...])

    pl.when(k == 0)(lambda: chunks(first))
    pl.when(jnp.logical_and(k > 0, k < last))(lambda: chunks(middle))
    pl.when(k == last)(lambda: chunks(final))


def _ffn_down(act, w_down, h1, g, beta):
    m, f = act.shape
    d = w_down.shape[1]
    bm, bk = 1024, 512
    assert m % bm == 0 and f % bk == 0 and f // bk >= 2
    vec = pl.BlockSpec((1, d), lambda i, k: (0, 0))
    return pl.pallas_call(
        _ffn_down_kernel,
        out_shape=jax.ShapeDtypeStruct((m, d), F32),
        grid=(m // bm, f // bk),
        in_specs=[
            pl.BlockSpec((bm, bk), lambda i, k: (i, k)),
            pl.BlockSpec((bk, d), lambda i, k: (k, 0)),
            pl.BlockSpec((bm, d), lambda i, k: (i, 0)),
            vec, vec,
        ],
        out_specs=pl.BlockSpec((bm, d), lambda i, k: (i, 0)),
        compiler_params=_cparams(2, 48),
        name="ffn_down",
    )(act, w_down, h1, g, beta)


def kernel(x, w_in, lambda_q1, lambda_k1, lambda_q2, lambda_k2, subln_g, w_out, ln1_g, ln1_b, w_up,
           conv_w, conv_b, w_down, ln2_g, ln2_b):
    batch, seq, d_model = x.shape
    depth = w_in.shape[0]
    mix_w = w_out.shape[1]
    diff_w = moba_w = mix_w // 2
    n_dh, n_mh = diff_w // HEAD_DIM, moba_w // HEAD_DIM
    assert w_in.shape[2] == 3 * diff_w + 3 * moba_w

    h = x.reshape(batch * seq, d_model)
    for l in range(depth):
        lam_init = 0.8 - 0.6 * math.exp(-0.3 * l)
        proj = _proj_rope(h, w_in[l].astype(BF16), seq, diff_w, moba_w)
        lam_params = [p[l].reshape(1, DIFF_SUB) for p in (lambda_q1, lambda_k1, lambda_q2, lambda_k2)]
        a_out = _diff_attn(proj, lam_params, subln_g[l].reshape(HEAD_DIM, 1), batch, seq, n_dh,
                           0, n_dh, 2 * n_dh, lam_init)
        b_out = _moba_attn(proj, batch, seq, n_mh, 3 * n_dh, 3 * n_dh + n_mh, 3 * n_dh + 2 * n_mh)
        h1, h1b = _out_ln1(a_out, b_out, w_out[l].astype(BF16), h,
                           ln1_g[l].reshape(1, d_model), ln1_b[l].reshape(1, d_model))
        act = _ffn_up(h1b, w_up[l], conv_w[l], conv_b[l].reshape(1, -1), seq)
        h = _ffn_down(act, w_down[l].astype(BF16), h1, ln2_g[l].reshape(1, d_model),
                      ln2_b[l].reshape(1, d_model))
    return h.reshape(batch, seq, d_model)
```

```python
import functools
import math

import jax
import jax.numpy as jnp
from jax import lax
from jax.experimental import pallas as pl
from jax.experimental.pallas import tpu as pltpu

F32 = jnp.float32
BF16 = jnp.bfloat16

LANES = 128
BF16_SUBLANES = 16
HEAD_DIM = 128
VT_ROWS = HEAD_DIM + BF16_SUBLANES
LOG2E = 1.4426950408889634
DIFF_SUB = HEAD_DIM // 2
ROPE_THETA = 500000.0
ROT_FRACTION = 4
MOBA_BLOCK = 256
MOBA_TOPK = 3
CONV_W = 3
CONV_HALO = 8
ROW_CHUNK = 256
LN_EPS = 1e-5
RMS_EPS = 1e-5
DEPTH = 1
DEEPNORM_ALPHA = (2.0 * DEPTH) ** 0.25
DIFF_SCALE = DIFF_SUB ** -0.5
MOBA_SCALE = HEAD_DIM ** -0.5
NEG_INF = float("-inf")

_NT = (((1,), (1,)), ((), ()))


def _cparams(n_axes, vmem_mib):
    return pltpu.CompilerParams(
        dimension_semantics=("arbitrary",) * n_axes,
        vmem_limit_bytes=vmem_mib * 1024 * 1024,
    )


def _rope_coeffs(seq, period):
    rot = period // ROT_FRACTION
    half = rot // 2
    inv = 1.0 / (ROPE_THETA ** (jnp.arange(0, rot, 2, dtype=F32) / rot))
    dim = [l % period for l in range(LANES)]
    lane_inv = inv[jnp.asarray([d % half for d in dim])]
    first = jnp.asarray([d < half for d in dim])[None, :]
    second = jnp.asarray([half <= d < rot for d in dim])[None, :]
    ang = jnp.arange(seq, dtype=F32)[:, None] * lane_inv[None, :]
    cos, sin = jnp.cos(ang), jnp.sin(ang)
    c = jnp.where(jnp.logical_or(first, second), cos, 1.0)
    s1 = jnp.where(first, -sin, 0.0)
    s2 = jnp.where(second, sin, 0.0)
    return jnp.stack([c, s1, s2], axis=0)


def _proj_rope_kernel(x_ref, w_ref, tab_ref, o_ref, xb_ref, *, diff_tiles, moba_tiles, q_scales,
                      n_heads_blk):
    j = pl.program_id(1)
    scale = jnp.float32(1.0)
    for t, s in q_scales:
        scale = jnp.where(j == t, jnp.float32(s), scale)

    def tile(half, first):
        for ch in range(x_ref.shape[0] // ROW_CHUNK):
            rows = slice(ch * ROW_CHUNK, (ch + 1) * ROW_CHUNK)
            if first:
                xb_ref[rows, :] = x_ref[rows, :].astype(BF16)
            acc = jnp.dot(xb_ref[rows, :], w_ref[...], preferred_element_type=F32)
            if half is None:
                o_ref[rows, :] = acc.astype(o_ref.dtype)
                continue
            c, s1, s2 = tab_ref[0, 0, rows, :], tab_ref[0, 1, rows, :], tab_ref[0, 2, rows, :]
            for hb in range(n_heads_blk):
                a = acc[:, hb * LANES:(hb + 1) * LANES]
                o = a * c + pltpu.roll(a, LANES - half, 1) * s1 + pltpu.roll(a, half, 1) * s2
                o_ref[rows, hb * LANES:(hb + 1) * LANES] = (o * scale).astype(o_ref.dtype)

    assert diff_tiles[0] == 0
    is_diff = functools.reduce(jnp.logical_or, [j == t for t in diff_tiles[1:]])
    is_moba = functools.reduce(jnp.logical_or, [j == t for t in moba_tiles])
    diff_half = DIFF_SUB // ROT_FRACTION // 2
    moba_half = HEAD_DIM // ROT_FRACTION // 2

    @pl.when(j == 0)
    def _():
        tile(diff_half, True)

    @pl.when(is_diff)
    def _():
        tile(diff_half, False)

    @pl.when(is_moba)
    def _():
        tile(moba_half, False)

    @pl.when(jnp.logical_not(jnp.logical_or(j == 0, jnp.logical_or(is_diff, is_moba))))
    def _():
        tile(None, False)


def _proj_rope(x2, w_in, seq, diff_w, moba_w):
    m, d = x2.shape
    n = w_in.shape[1]
    bm = min(1024, seq)
    bn = 1024
    assert seq % bm == 0 and m % bm == 0 and diff_w % bn == 0 and moba_w % bn == 0
    dt, mt = diff_w // bn, moba_w // bn
    q_d_tiles = list(range(0, dt))
    k_d_tiles = list(range(dt, 2 * dt))
    q_m_tiles = list(range(3 * dt, 3 * dt + mt))
    k_m_tiles = list(range(3 * dt + mt, 3 * dt + 2 * mt))
    tabs = jnp.stack([_rope_coeffs(seq, DIFF_SUB), _rope_coeffs(seq, HEAD_DIM)], axis=0)

    def kind(j):
        k = jnp.int32(0)
        for t in q_m_tiles + k_m_tiles:
            k = jnp.where(j == t, 1, k)
        return k

    spt = seq // bm
    q_scales = ([(t, DIFF_SCALE * LOG2E) for t in q_d_tiles] + [(t, MOBA_SCALE * LOG2E) for t in q_m_tiles])
    kern = functools.partial(_proj_rope_kernel, diff_tiles=q_d_tiles + k_d_tiles,
                             moba_tiles=q_m_tiles + k_m_tiles, q_scales=q_scales, n_heads_blk=bn // LANES)
    return pl.pallas_call(
        kern,
        out_shape=jax.ShapeDtypeStruct((m, n), BF16),
        grid=(m // bm, n // bn),
        in_specs=[
            pl.BlockSpec((bm, d), lambda i, j: (i, 0)),
            pl.BlockSpec((d, bn), lambda i, j: (0, j)),
            pl.BlockSpec((1, 3, bm, LANES), lambda i, j: (kind(j), 0, i % spt, 0)),
        ],
        out_specs=pl.BlockSpec((bm, bn), lambda i, j: (i, j)),
        scratch_shapes=[pltpu.VMEM((bm, d), BF16)],
        compiler_params=_cparams(2, 56),
        name="proj_rope",
    )(x2, w_in, tabs)


def _softmax_step(s_t, v_t, m_ref, acc_ref):
    m_prev = m_ref[...]
    m_new = jnp.maximum(m_prev, jnp.max(s_t, axis=0, keepdims=True))
    alpha = jnp.exp2(m_prev - m_new)
    p = jnp.exp2(s_t - m_new)
    acc_ref[...] = acc_ref[...] * alpha + jnp.dot(v_t, p.astype(v_t.dtype), preferred_element_type=F32)
    m_ref[...] = m_new


def _flash_pipeline(n_full, j_diag, diag_scores, full_scores, vt_ref, s_ref, slots, m_ref, acc_ref):
    sa, sb = slots
    m_ref[...] = jnp.full(m_ref.shape, NEG_INF, F32)
    acc_ref[...] = jnp.zeros(acc_ref.shape, F32)
    s_ref[sa] = diag_scores()
    n_pairs = (n_full + 1) // 2

    def pair(p, carry):
        s_ref[sb] = full_scores(2 * p)
        _softmax_step(s_ref[sa], vt_ref[jnp.where(p == 0, j_diag, 2 * p - 1)], m_ref, acc_ref)
        s_ref[sa] = full_scores(jnp.minimum(2 * p + 1, max(n_full - 1, 0)))
        _softmax_step(s_ref[sb], vt_ref[2 * p], m_ref, acc_ref)
        return carry

    def two_pairs(pp, carry):
        return pair(2 * pp + 1, pair(2 * pp, carry))

    if n_pairs >= 2:
        lax.fori_loop(0, n_pairs // 2, two_pairs, 0)
    if n_pairs % 2 == 1:
        pair(jnp.int32(n_pairs - 1), 0)
    if (n_full + 1) % 2 == 1:
        _softmax_step(s_ref[sa], vt_ref[j_diag if n_full == 0 else n_full - 1], m_ref, acc_ref)


def _store_v_transposed(v_ref, vt_ref, tk):
    pad = VT_ROWS - HEAD_DIM
    ones_row = (lax.broadcasted_iota(jnp.int32, (pad, tk), 0) == 0).astype(vt_ref.dtype)
    for j in range(vt_ref.shape[0]):
        vt_ref[j, 0:HEAD_DIM, :] = v_ref[j * tk:(j + 1) * tk, :].T
        vt_ref[j, HEAD_DIM:VT_ROWS, :] = ones_row


def _normalized(acc_ref):
    return acc_ref[0:HEAD_DIM, :] / acc_ref[HEAD_DIM:HEAD_DIM + 1, :]


def _diff_attn_kernel(lq1_ref, lk1_ref, lq2_ref, lk2_ref, g_ref, q_ref, k_ref, v_ref, o_ref,
                      qs_ref, vt_ref, s_ref, m_ref, acc_ref, *, tq, tk, lam_init):
    seq = q_ref.shape[0]
    _store_v_transposed(v_ref, vt_ref, tk)
    lam = (jnp.exp(jnp.sum(lq1_ref[...] * lk1_ref[...])) - jnp.exp(jnp.sum(lq2_ref[...] * lk2_ref[...]))
           + lam_init)

    for i in range(seq // tq):
        par = i % 2
        qs, m_par, acc_par = qs_ref.at[par], m_ref.at[par], acc_ref.at[par]
        q = q_ref[i * tq:(i + 1) * tq, :]
        lane = lax.broadcasted_iota(jnp.int32, q.shape, 1)
        zero = jnp.zeros_like(q)
        qs[0:tq, :] = jnp.where(lane < DIFF_SUB, q, zero)
        qs[tq:2 * tq, :] = jnp.where(lane >= DIFF_SUB, q, zero)

        def scores(j, qs=qs):
            k = k_ref[pl.ds(pl.multiple_of(j * tk, tk), tk), :]
            return lax.dot_general(k, qs[...], _NT, preferred_element_type=F32)

        j_diag = (i * tq) // tk

        def diag_scores(i=i, j_diag=j_diag, scores=scores, qs=qs):
            if tq != tk:
                s_t = scores(j_diag)
                kpos = j_diag * tk + lax.broadcasted_iota(jnp.int32, s_t.shape, 0)
                qpos = i * tq + (lax.broadcasted_iota(jnp.int32, s_t.shape, 1) & (tq - 1))
                return jnp.where(kpos <= qpos, s_t, NEG_INF)
            hq = tq // 2
            k = k_ref[j_diag * tk:(j_diag + 1) * tk, :]
            top = lax.dot_general(k[0:hq, :], qs[...], _NT, preferred_element_type=F32)
            r = lax.broadcasted_iota(jnp.int32, top.shape, 0)
            c = lax.broadcasted_iota(jnp.int32, top.shape, 1) & (tq - 1)
            top = jnp.where(r <= c, top, NEG_INF)
            q_late = jnp.concatenate([qs[hq:tq, :], qs[tq + hq:2 * tq, :]], axis=0)
            bot = lax.dot_general(k[hq:tq, :], q_late, _NT, preferred_element_type=F32)
            r = lax.broadcasted_iota(jnp.int32, bot.shape, 0)
            c = lax.broadcasted_iota(jnp.int32, bot.shape, 1) & (hq - 1)
            bot = jnp.where(r <= c, bot, NEG_INF)
            neg = jnp.full((hq, hq), NEG_INF, F32)
            bot = jnp.concatenate([neg, bot[:, 0:hq], neg, bot[:, hq:tq]], axis=1)
            return jnp.concatenate([top, bot], axis=0)

        _flash_pipeline(j_diag, j_diag, diag_scores, scores, vt_ref, s_ref, (2 * par, 2 * par + 1),
                        m_par, acc_par)

        o_t = _normalized(acc_par)
        o_t = o_t[:, 0:tq] - lam * o_t[:, tq:2 * tq]
        o_t = o_t * lax.rsqrt(jnp.mean(jnp.square(o_t), axis=0, keepdims=True) + RMS_EPS)
        o_t = o_t * g_ref[...] * (1.0 - lam_init)
        o_ref[i * tq:(i + 1) * tq, :] = o_t.T.astype(o_ref.dtype)


def _diff_attn(proj, lam_params, subln_g, batch, seq, n_heads, q_col, k_col, v_col, lam_init):
    tq = min(512, seq)
    tk = min(512, seq)
    assert seq % tk == 0 and tk % tq == 0 and (tq & (tq - 1)) == 0
    small = lambda n: pl.BlockSpec((1, n), lambda b, h: (0, 0))
    head = lambda col: pl.BlockSpec((seq, HEAD_DIM), lambda b, h: (b, col + h))
    kern = functools.partial(_diff_attn_kernel, tq=tq, tk=tk, lam_init=lam_init)
    return pl.pallas_call(
        kern,
        out_shape=jax.ShapeDtypeStruct((batch * seq, n_heads * HEAD_DIM), BF16),
        grid=(batch, n_heads),
        in_specs=[small(DIFF_SUB)] * 4 + [
            pl.BlockSpec((HEAD_DIM, 1), lambda b, h: (0, 0)),
            head(q_col), head(k_col), head(v_col),
        ],
        out_specs=head(0),
        scratch_shapes=[
            pltpu.VMEM((2, 2 * tq, HEAD_DIM), BF16),
            pltpu.VMEM((seq // tk, VT_ROWS, tk), BF16),
            pltpu.VMEM((4, tk, 2 * tq), F32),
            pltpu.VMEM((2, 1, 2 * tq), F32),
            pltpu.VMEM((2, VT_ROWS, 2 * tq), F32),
        ],
        compiler_params=_cparams(2, 40),
        name="diff_attn",
    )(*lam_params, subln_g, proj, proj, proj)


def _moba_attn_kernel(q_ref, k_ref, v_ref, o_ref, kmh_ref, kml_ref, vt_ref, sel_ref, s_ref, m_ref, acc_ref,
                      *, n_blocks, t):
    seq = q_ref.shape[0]
    blk = MOBA_BLOCK
    per = t // blk
    nbp = kmh_ref.shape[0]

    _store_v_transposed(v_ref, vt_ref, t)
    kmean = jnp.sum(k_ref[...].astype(F32).reshape(n_blocks, blk, HEAD_DIM), axis=1) * (1.0 / blk)
    if n_blocks < nbp:
        kmean = jnp.concatenate([kmean, jnp.zeros((nbp - n_blocks, HEAD_DIM), F32)], axis=0)
    hi = kmean.astype(BF16)
    kmh_ref[...] = hi
    kml_ref[...] = (kmean - hi.astype(F32)).astype(BF16)

    q_all = q_ref[...]
    gate = (lax.dot_general(kmh_ref[...], q_all, _NT, preferred_element_type=F32)
            + lax.dot_general(kml_ref[...], q_all, _NT, preferred_element_type=F32))
    blk_id = lax.broadcasted_iota(jnp.int32, (nbp, seq), 0)
    q_blk = lax.broadcasted_iota(jnp.int32, (nbp, seq), 1) // blk
    blk_f = blk_id.astype(F32)
    gate = jnp.where(blk_id < q_blk, gate, NEG_INF)
    sel = jnp.zeros((nbp, seq), F32)
    for _ in range(min(MOBA_TOPK, n_blocks)):
        mx = jnp.max(gate, axis=0, keepdims=True)
        cand = jnp.where(jnp.logical_and(gate == mx, gate > NEG_INF), blk_f, float(nbp))
        pick = blk_f == jnp.min(cand, axis=0, keepdims=True)
        sel = jnp.where(pick, 1.0, sel)
        gate = jnp.where(pick, NEG_INF, gate)
    sel_ref[...] = sel

    for i in range(seq // t):
        par = i % 2
        m_par, acc_par = m_ref.at[par], acc_ref.at[par]
        cols = slice(i * t, (i + 1) * t)

        def scores(g, cols=cols):
            k = k_ref[pl.ds(pl.multiple_of(g * t, t), t), :]
            return lax.dot_general(k, q_ref[cols, :], _NT, preferred_element_type=F32)

        def picked(g, cols=cols):
            rows = [jnp.broadcast_to(sel_ref[pl.ds(g * per + c, 1), cols] > 0.5, (blk, t))
                    for c in range(per)]
            return rows[0] if per == 1 else jnp.concatenate(rows, axis=0)

        def past_scores(g, scores=scores, picked=picked):
            return jnp.where(picked(g), scores(g), NEG_INF)

        def diag_scores(i=i, cols=cols, scores=scores, picked=picked):
            if per != 2:
                r = lax.broadcasted_iota(jnp.int32, (t, t), 0)
                c = lax.broadcasted_iota(jnp.int32, (t, t), 1)
                own_causal = jnp.logical_and(r // blk == c // blk, r <= c)
                return jnp.where(jnp.logical_or(own_causal, picked(i)), scores(i), NEG_INF)
            k = k_ref[i * t:(i + 1) * t, :]
            q = q_ref[cols, :]
            top = lax.dot_general(k[0:blk, :], q, _NT, preferred_element_type=F32)
            r = lax.broadcasted_iota(jnp.int32, top.shape, 0)
            c = lax.broadcasted_iota(jnp.int32, top.shape, 1)
            own_causal = jnp.logical_and(c < blk, r <= c)
            seen = jnp.logical_or(own_causal, jnp.broadcast_to(sel_ref[pl.ds(i * per, 1), cols] > 0.5, top.shape))
            top = jnp.where(seen, top, NEG_INF)
            bot = lax.dot_general(k[blk:t, :], q[blk:t, :], _NT, preferred_element_type=F32)
            r = lax.broadcasted_iota(jnp.int32, bot.shape, 0)
            c = lax.broadcasted_iota(jnp.int32, bot.shape, 1)
            bot = jnp.where(r <= c, bot, NEG_INF)
            bot = jnp.concatenate([jnp.full((blk, blk), NEG_INF, F32), bot], axis=1)
            return jnp.concatenate([top, bot], axis=0)

        _flash_pipeline(i, i, diag_scores, past_scores, vt_ref, s_ref, (2 * par, 2 * par + 1),
                        m_par, acc_par)
        o_ref[cols, :] = _normalized(acc_par).T.astype(o_ref.dtype)


def _moba_attn(proj, batch, seq, n_heads, q_col, k_col, v_col):
    t = min(2 * MOBA_BLOCK, seq)
    assert seq % t == 0 and t % MOBA_BLOCK == 0
    nb = seq // MOBA_BLOCK
    nbp = -(-nb // BF16_SUBLANES) * BF16_SUBLANES
    nt = seq // t
    head = lambda col: pl.BlockSpec((seq, HEAD_DIM), lambda b, h: (b, col + h))
    kern = functools.partial(_moba_attn_kernel, n_blocks=nb, t=t)
    return pl.pallas_call(
        kern,
        out_shape=jax.ShapeDtypeStruct((batch * seq, n_heads * HEAD_DIM), BF16),
        grid=(batch, n_heads),
        in_specs=[head(q_col), head(k_col), head(v_col)],
        out_specs=head(0),
        scratch_shapes=[
            pltpu.VMEM((nbp, HEAD_DIM), BF16),
            pltpu.VMEM((nbp, HEAD_DIM), BF16),
            pltpu.VMEM((nt, VT_ROWS, t), BF16),
            pltpu.VMEM((nbp, seq), F32),
            pltpu.VMEM((4, t, t), F32),
            pltpu.VMEM((2, 1, t), F32),
            pltpu.VMEM((2, VT_ROWS, t), F32),
        ],
        compiler_params=_cparams(2, 40),
        name="moba_attn",
    )(proj, proj, proj)


def _layer_norm(y, g, b):
    mu = jnp.mean(y, axis=-1, keepdims=True)
    yc = y - mu
    var = jnp.mean(jnp.square(yc), axis=-1, keepdims=True)
    return yc * lax.rsqrt(var + LN_EPS) * g + b


def _out_ln1_kernel(a_ref, b_ref, wa_ref, wb_ref, x_ref, g_ref, beta_ref, h_ref, hb_ref):
    for ch in range(x_ref.shape[0] // ROW_CHUNK):
        rows = slice(ch * ROW_CHUNK, (ch + 1) * ROW_CHUNK)
        mix = (jnp.dot(a_ref[rows, :], wa_ref[...], preferred_element_type=F32)
               + jnp.dot(b_ref[rows, :], wb_ref[...], preferred_element_type=F32))
        h = _layer_norm(DEEPNORM_ALPHA * x_ref[rows, :] + mix, g_ref[...], beta_ref[...])
        h_ref[rows, :] = h
        hb_ref[rows, :] = h.astype(hb_ref.dtype)


def _out_ln1(a_out, b_out, w_out, x2, g, beta):
    m, d = x2.shape
    wa, wb = a_out.shape[1], b_out.shape[1]
    bm = 512
    assert m % bm == 0
    row = lambda w: pl.BlockSpec((bm, w), lambda i: (i, 0))
    vec = pl.BlockSpec((1, d), lambda i: (0, 0))
    return pl.pallas_call(
        _out_ln1_kernel,
        out_shape=(jax.ShapeDtypeStruct((m, d), F32), jax.ShapeDtypeStruct((m, d), BF16)),
        grid=(m // bm,),
        in_specs=[
            row(wa), row(wb),
            pl.BlockSpec((wa, d), lambda i: (0, 0)),
            pl.BlockSpec((wb, d), lambda i: (wa // wb, 0)),
            row(d), vec, vec,
        ],
        out_specs=(row(d), row(d)),
        compiler_params=_cparams(1, 56),
        name="out_ln1",
    )(a_out, b_out, w_out, w_out, x2, g, beta)


def _ffn_up_kernel(h_ref, wg32_ref, wv32_ref, cw_ref, cb_ref, o_ref, wg_ref, wv_ref, gbuf_ref,
                   *, bm, tiles_per_seq):
    r = pl.program_id(1)

    @pl.when(r == 0)
    def _():
        wg_ref[...] = wg32_ref[...].astype(BF16)
        wv_ref[...] = wv32_ref[...].astype(BF16)

    @pl.when(r % tiles_per_seq == 0)
    def _():
        gbuf_ref[0:CONV_HALO, :] = jnp.zeros((CONV_HALO, gbuf_ref.shape[1]), F32)

    cw = cw_ref[...]
    for c in range(bm // ROW_CHUNK):
        lo = c * ROW_CHUNK
        h = h_ref[lo:lo + ROW_CHUNK, :]
        g = jnp.dot(h, wg_ref[...], preferred_element_type=F32)
        val = jnp.dot(h, wv_ref[...], preferred_element_type=F32)
        gbuf_ref[CONV_HALO + lo:CONV_HALO + lo + ROW_CHUNK, :] = g
        g1 = gbuf_ref[pl.ds(CONV_HALO + lo - 1, ROW_CHUNK), :]
        g2 = gbuf_ref[pl.ds(CONV_HALO + lo - 2, ROW_CHUNK), :]
        gc = cb_ref[...] + (cw[0:1, :] * g2 + cw[1:2, :] * g1 + cw[2:3, :] * g)
        act = gc * jax.nn.sigmoid(gc) * val
        o_ref[lo:lo + ROW_CHUNK, :] = act.astype(o_ref.dtype)
    gbuf_ref[0:CONV_HALO, :] = gbuf_ref[bm:bm + CONV_HALO, :]


def _ffn_up(hb, w_up, conv_w, conv_b, seq):
    m, d = hb.shape
    f = conv_w.shape[1]
    bm = min(2048, seq)
    bn = 512
    assert seq % bm == 0 and f % bn == 0
    nct = f // bn
    kern = functools.partial(_ffn_up_kernel, bm=bm, tiles_per_seq=seq // bm)
    return pl.pallas_call(
        kern,
        out_shape=jax.ShapeDtypeStruct((m, f), BF16),
        grid=(nct, m // bm),
        in_specs=[
            pl.BlockSpec((bm, d), lambda c, r: (r, 0)),
            pl.BlockSpec((d, bn), lambda c, r: (0, c)),
            pl.BlockSpec((d, bn), lambda c, r: (0, nct + c)),
            pl.BlockSpec((CONV_W, bn), lambda c, r: (0, c)),
            pl.BlockSpec((1, bn), lambda c, r: (0, c)),
        ],
        out_specs=pl.BlockSpec((bm, bn), lambda c, r: (r, c)),
        scratch_shapes=[pltpu.VMEM((d, bn), BF16), pltpu.VMEM((d, bn), BF16),
                        pltpu.VMEM((bm + CONV_HALO, bn), F32)],
        compiler_params=_cparams(2, 54),
        name="ffn_up",
    )(hb, w_up, w_up, conv_w, conv_b)


def _ffn_down_kernel(a_ref, w_ref, h_ref, g_ref, beta_ref, o_ref):
    k = pl.program_id(1)
    last = pl.num_programs(1) - 1

    def chunks(step):
        for ch in range(o_ref.shape[0] // ROW_CHUNK):
            rows = slice(ch * ROW_CHUNK, (ch + 1) * ROW_CHUNK)
            step(rows, jnp.dot(a_ref[rows, :], w_ref[...], preferred_element_type=F32))

    def first(rows, part):
        o_ref[rows, :] = DEEPNORM_ALPHA * h_ref[rows, :] + part

    def middle(rows, part):
        o_ref[rows, :] += part

    def final(rows, part):
        o_ref[rows, :] = _layer_norm(o_ref[rows, :] + part, g_ref[...], beta_ref[...])

    pl.when(k == 0)(lambda: chunks(first))
    pl.when(jnp.logical_and(k > 0, k < last))(lambda: chunks(middle))
    pl.when(k == last)(lambda: chunks(final))


def _ffn_down(act, w_down, h1, g, beta):
    m, f = act.shape
    d = w_down.shape[1]
    bm, bk = 1024, 512
    assert m % bm == 0 and f % bk == 0 and f // bk >= 2
    vec = pl.BlockSpec((1, d), lambda i, k: (0, 0))
    return pl.pallas_call(
        _ffn_down_kernel,
        out_shape=jax.ShapeDtypeStruct((m, d), F32),
        grid=(m // bm, f // bk),
        in_specs=[
            pl.BlockSpec((bm, bk), lambda i, k: (i, k)),
            pl.BlockSpec((bk, d), lambda i, k: (k, 0)),
            pl.BlockSpec((bm, d), lambda i, k: (i, 0)),
            vec, vec,
        ],
        out_specs=pl.BlockSpec((bm, d), lambda i, k: (i, 0)),
        compiler_params=_cparams(2, 48),
        name="ffn_down",
    )(act, w_down, h1, g, beta)


def kernel(x, w_in, lambda_q1, lambda_k1, lambda_q2, lambda_k2, subln_g, w_out, ln1_g, ln1_b, w_up,
           conv_w, conv_b, w_down, ln2_g, ln2_b):
    batch, seq, d_model = x.shape
    depth = w_in.shape[0]
    mix_w = w_out.shape[1]
    diff_w = moba_w = mix_w // 2
    n_dh, n_mh = diff_w // HEAD_DIM, moba_w // HEAD_DIM
    assert w_in.shape[2] == 3 * diff_w + 3 * moba_w

    h = x.reshape(batch * seq, d_model)
    for l in range(depth):
        lam_init = 0.8 - 0.6 * math.exp(-0.3 * l)
        proj = _proj_rope(h, w_in[l].astype(BF16), seq, diff_w, moba_w)
        lam_params = [p[l].reshape(1, DIFF_SUB) for p in (lambda_q1, lambda_k1, lambda_q2, lambda_k2)]
        a_out = _diff_attn(proj, lam_params, subln_g[l].reshape(HEAD_DIM, 1), batch, seq, n_dh,
                           0, n_dh, 2 * n_dh, lam_init)
        b_out = _moba_attn(proj, batch, seq, n_mh, 3 * n_dh, 3 * n_dh + n_mh, 3 * n_dh + 2 * n_mh)
        h1, h1b = _out_ln1(a_out, b_out, w_out[l].astype(BF16), h,
                           ln1_g[l].reshape(1, d_model), ln1_b[l].reshape(1, d_model))
        act = _ffn_up(h1b, w_up[l], conv_w[l], conv_b[l].reshape(1, -1), seq)
        h = _ffn_down(act, w_down[l].astype(BF16), h1, ln2_g[l].reshape(1, d_model),
                      ln2_b[l].reshape(1, d_model))
    return h.reshape(batch, seq, d_model)
```

```python
import functools
import math

import jax
import jax.numpy as jnp
from jax import lax
from jax.experimental import pallas as pl
from jax.experimental.pallas import tpu as pltpu

F32 = jnp.float32
BF16 = jnp.bfloat16

LANES = 128
BF16_SUBLANES = 16
HEAD_DIM = 128
VT_ROWS = HEAD_DIM + BF16_SUBLANES
LOG2E = 1.4426950408889634
DIFF_SUB = HEAD_DIM // 2
ROPE_THETA = 500000.0
ROT_FRACTION = 4
MOBA_BLOCK = 256
MOBA_TOPK = 3
CONV_W = 3
CONV_HALO = 8
ROW_CHUNK = 256
LN_EPS = 1e-5
RMS_EPS = 1e-5
DEPTH = 1
DEEPNORM_ALPHA = (2.0 * DEPTH) ** 0.25
DIFF_SCALE = DIFF_SUB ** -0.5
MOBA_SCALE = HEAD_DIM ** -0.5
NEG_INF = float("-inf")

_NT = (((1,), (1,)), ((), ()))


def _cparams(n_axes, vmem_mib):
    return pltpu.CompilerParams(
        dimension_semantics=("arbitrary",) * n_axes,
        vmem_limit_bytes=vmem_mib * 1024 * 1024,
    )


def _rope_coeffs(seq, period):
    rot = period // ROT_FRACTION
    half = rot // 2
    inv = 1.0 / (ROPE_THETA ** (jnp.arange(0, rot, 2, dtype=F32) / rot))
    dim = [l % period for l in range(LANES)]
    lane_inv = inv[jnp.asarray([d % half for d in dim])]
    first = jnp.asarray([d < half for d in dim])[None, :]
    second = jnp.asarray([half <= d < rot for d in dim])[None, :]
    ang = jnp.arange(seq, dtype=F32)[:, None] * lane_inv[None, :]
    cos, sin = jnp.cos(ang), jnp.sin(ang)
    c = jnp.where(jnp.logical_or(first, second), cos, 1.0)
    s1 = jnp.where(first, -sin, 0.0)
    s2 = jnp.where(second, sin, 0.0)
    return jnp.stack([c, s1, s2], axis=0)


def _proj_rope_kernel(x_ref, w_ref, tab_ref, o_ref, xb_ref, *, diff_tiles, moba_tiles, q_scales,
                      n_heads_blk):
    j = pl.program_id(1)
    scale = jnp.float32(1.0)
    for t, s in q_scales:
        scale = jnp.where(j == t, jnp.float32(s), scale)

    def tile(half, first):
        for ch in range(x_ref.shape[0] // ROW_CHUNK):
            rows = slice(ch * ROW_CHUNK, (ch + 1) * ROW_CHUNK)
            if first:
                xb_ref[rows, :] = x_ref[rows, :].astype(BF16)
            acc = jnp.dot(xb_ref[rows, :], w_ref[...], preferred_element_type=F32)
            if half is None:
                o_ref[rows, :] = acc.astype(o_ref.dtype)
                continue
            c, s1, s2 = tab_ref[0, 0, rows, :], tab_ref[0, 1, rows, :], tab_ref[0, 2, rows, :]
            for hb in range(n_heads_blk):
                a = acc[:, hb * LANES:(hb + 1) * LANES]
                o = a * c + pltpu.roll(a, LANES - half, 1) * s1 + pltpu.roll(a, half, 1) * s2
                o_ref[rows, hb * LANES:(hb + 1) * LANES] = (o * scale).astype(o_ref.dtype)

    assert diff_tiles[0] == 0
    is_diff = functools.reduce(jnp.logical_or, [j == t for t in diff_tiles[1:]])
    is_moba = functools.reduce(jnp.logical_or, [j == t for t in moba_tiles])
    diff_half = DIFF_SUB // ROT_FRACTION // 2
    moba_half = HEAD_DIM // ROT_FRACTION // 2

    @pl.when(j == 0)
    def _():
        tile(diff_half, True)

    @pl.when(is_diff)
    def _():
        tile(diff_half, False)

    @pl.when(is_moba)
    def _():
        tile(moba_half, False)

    @pl.when(jnp.logical_not(jnp.logical_or(j == 0, jnp.logical_or(is_diff, is_moba))))
    def _():
        tile(None, False)


def _proj_rope(x2, w_in, seq, diff_w, moba_w):
    m, d = x2.shape
    n = w_in.shape[1]
    bm = min(1024, seq)
    bn = 1024
    assert seq % bm == 0 and m % bm == 0 and diff_w % bn == 0 and moba_w % bn == 0
    dt, mt = diff_w // bn, moba_w // bn
    q_d_tiles = list(range(0, dt))
    k_d_tiles = list(range(dt, 2 * dt))
    q_m_tiles = list(range(3 * dt, 3 * dt + mt))
    k_m_tiles = list(range(3 * dt + mt, 3 * dt + 2 * mt))
    tabs = jnp.stack([_rope_coeffs(seq, DIFF_SUB), _rope_coeffs(seq, HEAD_DIM)], axis=0)

    def kind(j):
        k = jnp.int32(0)
        for t in q_m_tiles + k_m_tiles:
            k = jnp.where(j == t, 1, k)
        return k

    spt = seq // bm
    q_scales = ([(t, DIFF_SCALE * LOG2E) for t in q_d_tiles] + [(t, MOBA_SCALE * LOG2E) for t in q_m_tiles])
    kern = functools.partial(_proj_rope_kernel, diff_tiles=q_d_tiles + k_d_tiles,
                             moba_tiles=q_m_tiles + k_m_tiles, q_scales=q_scales, n_heads_blk=bn // LANES)
    return pl.pallas_call(
        kern,
        out_shape=jax.ShapeDtypeStruct((m, n), BF16),
        grid=(m // bm, n // bn),
        in_specs=[
            pl.BlockSpec((bm, d), lambda i, j: (i, 0)),
            pl.BlockSpec((d, bn), lambda i, j: (0, j)),
            pl.BlockSpec((1, 3, bm, LANES), lambda i, j: (kind(j), 0, i % spt, 0)),
        ],
        out_specs=pl.BlockSpec((bm, bn), lambda i, j: (i, j)),
        scratch_shapes=[pltpu.VMEM((bm, d), BF16)],
        compiler_params=_cparams(2, 56),
        name="proj_rope",
    )(x2, w_in, tabs)


def _softmax_step(s_t, v_t, m_ref, acc_ref):
    m_prev = m_ref[...]
    m_new = jnp.maximum(m_prev, jnp.max(s_t, axis=0, keepdims=True))
    alpha = jnp.exp2(m_prev - m_new)
    p = jnp.exp2(s_t - m_new)
    acc_ref[...] = acc_ref[...] * alpha + jnp.dot(v_t, p.astype(v_t.dtype), preferred_element_type=F32)
    m_ref[...] = m_new


def _flash_pipeline(n_full, j_diag, diag_scores, full_scores, vt_ref, s_ref, slots, m_ref, acc_ref):
    sa, sb = slots
    m_ref[...] = jnp.full(m_ref.shape, NEG_INF, F32)
    acc_ref[...] = jnp.zeros(acc_ref.shape, F32)
    s_ref[sa] = diag_scores()
    n_pairs = (n_full + 1) // 2

    def pair(p, carry):
        s_ref[sb] = full_scores(2 * p)
        _softmax_step(s_ref[sa], vt_ref[jnp.where(p == 0, j_diag, 2 * p - 1)], m_ref, acc_ref)
        s_ref[sa] = full_scores(jnp.minimum(2 * p + 1, max(n_full - 1, 0)))
        _softmax_step(s_ref[sb], vt_ref[2 * p], m_ref, acc_ref)
        return carry

    def two_pairs(pp, carry):
        return pair(2 * pp + 1, pair(2 * pp, carry))

    if n_pairs >= 2:
        lax.fori_loop(0, n_pairs // 2, two_pairs, 0)
    if n_pairs % 2 == 1:
        pair(jnp.int32(n_pairs - 1), 0)
    if (n_full + 1) % 2 == 1:
        _softmax_step(s_ref[sa], vt_ref[j_diag if n_full == 0 else n_full - 1], m_ref, acc_ref)


def _store_v_transposed(v_ref, vt_ref, tk):
    pad = VT_ROWS - HEAD_DIM
    ones_row = (lax.broadcasted_iota(jnp.int32, (pad, tk), 0) == 0).astype(vt_ref.dtype)
    for j in range(vt_ref.shape[0]):
        vt_ref[j, 0:HEAD_DIM, :] = v_ref[j * tk:(j + 1) * tk, :].T
        vt_ref[j, HEAD_DIM:VT_ROWS, :] = ones_row


def _normalized(acc_ref):
    return acc_ref[0:HEAD_DIM, :] / acc_ref[HEAD_DIM:HEAD_DIM + 1, :]


def _diff_attn_kernel(lq1_ref, lk1_ref, lq2_ref, lk2_ref, g_ref, q_ref, k_ref, v_ref, o_ref,
                      qs_ref, vt_ref, s_ref, m_ref, acc_ref, *, tq, tk, lam_init):
    seq = q_ref.shape[0]
    _store_v_transposed(v_ref, vt_ref, tk)
    lam = (jnp.exp(jnp.sum(lq1_ref[...] * lk1_ref[...])) - jnp.exp(jnp.sum(lq2_ref[...] * lk2_ref[...]))
           + lam_init)

    for i in range(seq // tq):
        par = i % 2
        qs, m_par, acc_par = qs_ref.at[par], m_ref.at[par], acc_ref.at[par]
        q = q_ref[i * tq:(i + 1) * tq, :]
        lane = lax.broadcasted_iota(jnp.int32, q.shape, 1)
        zero = jnp.zeros_like(q)
        qs[0:tq, :] = jnp.where(lane < DIFF_SUB, q, zero)
        qs[tq:2 * tq, :] = jnp.where(lane >= DIFF_SUB, q, zero)

        def scores(j, qs=qs):
            k = k_ref[pl.ds(pl.multiple_of(j * tk, tk), tk), :]
            return lax.dot_general(k, qs[...], _NT, preferred_element_type=F32)

        j_diag = (i * tq) // tk

        def diag_scores(i=i, j_diag=j_diag, scores=scores, qs=qs):
            if tq != tk:
                s_t = scores(j_diag)
                kpos = j_diag * tk + lax.broadcasted_iota(jnp.int32, s_t.shape, 0)
                qpos = i * tq + (lax.broadcasted_iota(jnp.int32, s_t.shape, 1) & (tq - 1))
                return jnp.where(kpos <= qpos, s_t, NEG_INF)
            hq = tq // 2
            k = k_ref[j_diag * tk:(j_diag + 1) * tk, :]
            top = lax.dot_general(k[0:hq, :], qs[...], _NT, preferred_element_type=F32)
            r = lax.broadcasted_iota(jnp.int32, top.shape, 0)
            c = lax.broadcasted_iota(jnp.int32, top.shape, 1) & (tq - 1)
            top = jnp.where(r <= c, top, NEG_INF)
            q_late = jnp.concatenate([qs[hq:tq, :], qs[tq + hq:2 * tq, :]], axis=0)
            bot = lax.dot_general(k[hq:tq, :], q_late, _NT, preferred_element_type=F32)
            r = lax.broadcasted_iota(jnp.int32, bot.shape, 0)
            c = lax.broadcasted_iota(jnp.int32, bot.shape, 1) & (hq - 1)
            bot = jnp.where(r <= c, bot, NEG_INF)
            neg = jnp.full((hq, hq), NEG_INF, F32)
            bot = jnp.concatenate([neg, bot[:, 0:hq], neg, bot[:, hq:tq]], axis=1)
            return jnp.concatenate([top, bot], axis=0)

        _flash_pipeline(j_diag, j_diag, diag_scores, scores, vt_ref, s_ref, (2 * par, 2 * par + 1),
                        m_par, acc_par)

        o_t = _normalized(acc_par)
        o_t = o_t[:, 0:tq] - lam * o_t[:, tq:2 * tq]
        o_t = o_t * lax.rsqrt(jnp.mean(jnp.square(o_t), axis=0, keepdims=True) + RMS_EPS)
        o_t = o_t * g_ref[...] * (1.0 - lam_init)
        o_ref[i * tq:(i + 1) * tq, :] = o_t.T.astype(o_ref.dtype)


def _diff_attn(proj, lam_params, subln_g, batch, seq, n_heads, q_col, k_col, v_col, lam_init):
    tq = min(512, seq)
    tk = min(512, seq)
    assert seq % tk == 0 and tk % tq == 0 and (tq & (tq - 1)) == 0
    small = lambda n: pl.BlockSpec((1, n), lambda b, h: (0, 0))
    head = lambda col: pl.BlockSpec((seq, HEAD_DIM), lambda b, h: (b, col + h))
    kern = functools.partial(_diff_attn_kernel, tq=tq, tk=tk, lam_init=lam_init)
    return pl.pallas_call(
        kern,
        out_shape=jax.ShapeDtypeStruct((batch * seq, n_heads * HEAD_DIM), BF16),
        grid=(batch, n_heads),
        in_specs=[small(DIFF_SUB)] * 4 + [
            pl.BlockSpec((HEAD_DIM, 1), lambda b, h: (0, 0)),
            head(q_col), head(k_col), head(v_col),
        ],
        out_specs=head(0),
        scratch_shapes=[
            pltpu.VMEM((2, 2 * tq, HEAD_DIM), BF16),
            pltpu.VMEM((seq // tk, VT_ROWS, tk), BF16),
            pltpu.VMEM((4, tk, 2 * tq), F32),
            pltpu.VMEM((2, 1, 2 * tq), F32),
            pltpu.VMEM((2, VT_ROWS, 2 * tq), F32),
        ],
        compiler_params=_cparams(2, 40),
        name="diff_attn",
    )(*lam_params, subln_g, proj, proj, proj)


def _moba_attn_kernel(q_ref, k_ref, v_ref, o_ref, kmh_ref, kml_ref, vt_ref, sel_ref, s_ref, m_ref, acc_ref,
                      *, n_blocks, t):
    seq = q_ref.shape[0]
    blk = MOBA_BLOCK
    per = t // blk
    nbp = kmh_ref.shape[0]

    _store_v_transposed(v_ref, vt_ref, t)
    kmean = jnp.sum(k_ref[...].astype(F32).reshape(n_blocks, blk, HEAD_DIM), axis=1) * (1.0 / blk)
    if n_blocks < nbp:
        kmean = jnp.concatenate([kmean, jnp.zeros((nbp - n_blocks, HEAD_DIM), F32)], axis=0)
    hi = kmean.astype(BF16)
    kmh_ref[...] = hi
    kml_ref[...] = (kmean - hi.astype(F32)).astype(BF16)

    q_all = q_ref[...]
    gate = (lax.dot_general(kmh_ref[...], q_all, _NT, preferred_element_type=F32)
            + lax.dot_general(kml_ref[...], q_all, _NT, preferred_element_type=F32))
    blk_id = lax.broadcasted_iota(jnp.int32, (nbp, seq), 0)
    q_blk = lax.broadcasted_iota(jnp.int32, (nbp, seq), 1) // blk
    blk_f = blk_id.astype(F32)
    gate = jnp.where(blk_id < q_blk, gate, NEG_INF)
    sel = jnp.zeros((nbp, seq), F32)
    for _ in range(min(MOBA_TOPK, n_blocks)):
        mx = jnp.max(gate, axis=0, keepdims=True)
        cand = jnp.where(jnp.logical_and(gate == mx, gate > NEG_INF), blk_f, float(nbp))
        pick = blk_f == jnp.min(cand, axis=0, keepdims=True)
        sel = jnp.where(pick, 1.0, sel)
        gate = jnp.where(pick, NEG_INF, gate)
    sel_ref[...] = sel

    for i in range(seq // t):
        par = i % 2
        m_par, acc_par = m_ref.at[par], acc_ref.at[par]
        cols = slice(i * t, (i + 1) * t)

        def scores(g, cols=cols):
            k = k_ref[pl.ds(pl.multiple_of(g * t, t), t), :]
            return lax.dot_general(k, q_ref[cols, :], _NT, preferred_element_type=F32)

        def picked(g, cols=cols):
            rows = [jnp.broadcast_to(sel_ref[pl.ds(g * per + c, 1), cols] > 0.5, (blk, t))
                    for c in range(per)]
            return rows[0] if per == 1 else jnp.concatenate(rows, axis=0)

        def past_scores(g, scores=scores, picked=picked):
            return jnp.where(picked(g), scores(g), NEG_INF)

        def diag_scores(i=i, cols=cols, scores=scores, picked=picked):
            if per != 2:
                r = lax.broadcasted_iota(jnp.int32, (t, t), 0)
                c = lax.broadcasted_iota(jnp.int32, (t, t), 1)
                own_causal = jnp.logical_and(r // blk == c // blk, r <= c)
                return jnp.where(jnp.logical_or(own_causal, picked(i)), scores(i), NEG_INF)
            k = k_ref[i * t:(i + 1) * t, :]
            q = q_ref[cols, :]
            top = lax.dot_general(k[0:blk, :], q, _NT, preferred_element_type=F32)
            r = lax.broadcasted_iota(jnp.int32, top.shape, 0)
            c = lax.broadcasted_iota(jnp.int32, top.shape, 1)
            own_causal = jnp.logical_and(c < blk, r <= c)
            seen = jnp.logical_or(own_causal, jnp.broadcast_to(sel_ref[pl.ds(i * per, 1), cols] > 0.5, top.shape))
            top = jnp.where(seen, top, NEG_INF)
            bot = lax.dot_general(k[blk:t, :], q[blk:t, :], _NT, preferred_element_type=F32)
            r = lax.broadcasted_iota(jnp.int32, bot.shape, 0)
            c = lax.broadcasted_iota(jnp.int32, bot.shape, 1)
            bot = jnp.where(r <= c, bot, NEG_INF)
            bot = jnp.concatenate([jnp.full((blk, blk), NEG_INF, F32), bot], axis=1)
            return jnp.concatenate([top, bot], axis=0)

        _flash_pipeline(i, i, diag_scores, past_scores, vt_ref, s_ref, (2 * par, 2 * par + 1),
                        m_par, acc_par)
        o_ref[cols, :] = _normalized(acc_par).T.astype(o_ref.dtype)


def _moba_attn(proj, batch, seq, n_heads, q_col, k_col, v_col):
    t = min(2 * MOBA_BLOCK, seq)
    assert seq % t == 0 and t % MOBA_BLOCK == 0
    nb = seq // MOBA_BLOCK
    nbp = -(-nb // BF16_SUBLANES) * BF16_SUBLANES
    nt = seq // t
    head = lambda col: pl.BlockSpec((seq, HEAD_DIM), lambda b, h: (b, col + h))
    kern = functools.partial(_moba_attn_kernel, n_blocks=nb, t=t)
    return pl.pallas_call(
        kern,
        out_shape=jax.ShapeDtypeStruct((batch * seq, n_heads * HEAD_DIM), BF16),
        grid=(batch, n_heads),
        in_specs=[head(q_col), head(k_col), head(v_col)],
        out_specs=head(0),
        scratch_shapes=[
            pltpu.VMEM((nbp, HEAD_DIM), BF16),
            pltpu.VMEM((nbp, HEAD_DIM), BF16),
            pltpu.VMEM((nt, VT_ROWS, t), BF16),
            pltpu.VMEM((nbp, seq), F32),
            pltpu.VMEM((4, t, t), F32),
            pltpu.VMEM((2, 1, t), F32),
            pltpu.VMEM((2, VT_ROWS, t), F32),
        ],
        compiler_params=_cparams(2, 40),
        name="moba_attn",
    )(proj, proj, proj)


def _layer_norm(y, g, b):
    mu = jnp.mean(y, axis=-1, keepdims=True)
    yc = y - mu
    var = jnp.mean(jnp.square(yc), axis=-1, keepdims=True)
    return yc * lax.rsqrt(var + LN_EPS) * g + b


def _out_ln1_kernel(a_ref, b_ref, wa_ref, wb_ref, x_ref, g_ref, beta_ref, h_ref, hb_ref):
    for ch in range(x_ref.shape[0] // ROW_CHUNK):
        rows = slice(ch * ROW_CHUNK, (ch + 1) * ROW_CHUNK)
        mix = (jnp.dot(a_ref[rows, :], wa_ref[...], preferred_element_type=F32)
               + jnp.dot(b_ref[rows, :], wb_ref[...], preferred_element_type=F32))
        h = _layer_norm(DEEPNORM_ALPHA * x_ref[rows, :] + mix, g_ref[...], beta_ref[...])
        h_ref[rows, :] = h
        hb_ref[rows, :] = h.astype(hb_ref.dtype)


def _out_ln1(a_out, b_out, w_out, x2, g, beta):
    m, d = x2.shape
    wa, wb = a_out.shape[1], b_out.shape[1]
    bm = 512
    assert m % bm == 0
    row = lambda w: pl.BlockSpec((bm, w), lambda i: (i, 0))
    vec = pl.BlockSpec((1, d), lambda i: (0, 0))
    return pl.pallas_call(
        _out_ln1_kernel,
        out_shape=(jax.ShapeDtypeStruct((m, d), F32), jax.ShapeDtypeStruct((m, d), BF16)),
        grid=(m // bm,),
        in_specs=[
            row(wa), row(wb),
            pl.BlockSpec((wa, d), lambda i: (0, 0)),
            pl.BlockSpec((wb, d), lambda i: (wa // wb, 0)),
            row(d), vec, vec,
        ],
        out_specs=(row(d), row(d)),
        compiler_params=_cparams(1, 56),
        name="out_ln1",
    )(a_out, b_out, w_out, w_out, x2, g, beta)


def _ffn_up_kernel(h_ref, wg32_ref, wv32_ref, cw_ref, cb_ref, o_ref, wg_ref, wv_ref, gbuf_ref,
                   *, bm, tiles_per_seq):
    r = pl.program_id(1)

    @pl.when(r == 0)
    def _():
        wg_ref[...] = wg32_ref[...].astype(BF16)
        wv_ref[...] = wv32_ref[...].astype(BF16)

    @pl.when(r % tiles_per_seq == 0)
    def _():
        gbuf_ref[0:CONV_HALO, :] = jnp.zeros((CONV_HALO, gbuf_ref.shape[1]), F32)

    cw = cw_ref[...]
    for c in range(bm // ROW_CHUNK):
        lo = c * ROW_CHUNK
        h = h_ref[lo:lo + ROW_CHUNK, :]
        g = jnp.dot(h, wg_ref[...], preferred_element_type=F32)
        val = jnp.dot(h, wv_ref[...], preferred_element_type=F32)
        gbuf_ref[CONV_HALO + lo:CONV_HALO + lo + ROW_CHUNK, :] = g
        g1 = gbuf_ref[pl.ds(CONV_HALO + lo - 1, ROW_CHUNK), :]
        g2 = gbuf_ref[pl.ds(CONV_HALO + lo - 2, ROW_CHUNK), :]
        gc = cb_ref[...] + (cw[0:1, :] * g2 + cw[1:2, :] * g1 + cw[2:3, :] * g)
        act = gc * jax.nn.sigmoid(gc) * val
        o_ref[lo:lo + ROW_CHUNK, :] = act.astype(o_ref.dtype)
    gbuf_ref[0:CONV_HALO, :] = gbuf_ref[bm:bm + CONV_HALO, :]


def _ffn_up(hb, w_up, conv_w, conv_b, seq):
    m, d = hb.shape
    f = conv_w.shape[1]
    bm = min(2048, seq)
    bn = 512
    assert seq % bm == 0 and f % bn == 0
    nct = f // bn
    kern = functools.partial(_ffn_up_kernel, bm=bm, tiles_per_seq=seq // bm)
    return pl.pallas_call(
        kern,
        out_shape=jax.ShapeDtypeStruct((m, f), BF16),
        grid=(nct, m // bm),
        in_specs=[
            pl.BlockSpec((bm, d), lambda c, r: (r, 0)),
            pl.BlockSpec((d, bn), lambda c, r: (0, c)),
            pl.BlockSpec((d, bn), lambda c, r: (0, nct + c)),
            pl.BlockSpec((CONV_W, bn), lambda c, r: (0, c)),
            pl.BlockSpec((1, bn), lambda c, r: (0, c)),
        ],
        out_specs=pl.BlockSpec((bm, bn), lambda c, r: (r, c)),
        scratch_shapes=[pltpu.VMEM((d, bn), BF16), pltpu.VMEM((d, bn), BF16),
                        pltpu.VMEM((bm + CONV_HALO, bn), F32)],
        compiler_params=_cparams(2, 54),
        name="ffn_up",
    )(hb, w_up, w_up, conv_w, conv_b)


def _ffn_down_kernel(a_ref, w_ref, h_ref, g_ref, beta_ref, o_ref):
    for ch in range(o_ref.shape[0] // ROW_CHUNK):
        rows = slice(ch * ROW_CHUNK, (ch + 1) * ROW_CHUNK)
        f = jnp.dot(a_ref[rows, :], w_ref[...], preferred_element_type=F32)
        o_ref[rows, :] = _layer_norm(DEEPNORM_ALPHA * h_ref[rows, :] + f, g_ref[...], beta_ref[...])


def _ffn_down(act, w_down, h1, g, beta):
    m, f = act.shape
    d = w_down.shape[1]
    bm = 512
    assert m % bm == 0
    vec = pl.BlockSpec((1, d), lambda i: (0, 0))
    return pl.pallas_call(
        _ffn_down_kernel,
        out_shape=jax.ShapeDtypeStruct((m, d), F32),
        grid=(m // bm,),
        in_specs=[
            pl.BlockSpec((bm, f), lambda i: (i, 0)),
            pl.BlockSpec((f, d), lambda i: (0, 0), pipeline_mode=pl.Buffered(1)),
            pl.BlockSpec((bm, d), lambda i: (i, 0)),
            vec, vec,
        ],
        out_specs=pl.BlockSpec((bm, d), lambda i: (i, 0)),
        compiler_params=_cparams(1, 56),
        name="ffn_down",
    )(act, w_down, h1, g, beta)


def kernel(x, w_in, lambda_q1, lambda_k1, lambda_q2, lambda_k2, subln_g, w_out, ln1_g, ln1_b, w_up,
           conv_w, conv_b, w_down, ln2_g, ln2_b):
    batch, seq, d_model = x.shape
    depth = w_in.shape[0]
    mix_w = w_out.shape[1]
    diff_w = moba_w = mix_w // 2
    n_dh, n_mh = diff_w // HEAD_DIM, moba_w // HEAD_DIM
    assert w_in.shape[2] == 3 * diff_w + 3 * moba_w

    h = x.reshape(batch * seq, d_model)
    for l in range(depth):
        lam_init = 0.8 - 0.6 * math.exp(-0.3 * l)
        proj = _proj_rope(h, w_in[l].astype(BF16), seq, diff_w, moba_w)
        lam_params = [p[l].reshape(1, DIFF_SUB) for p in (lambda_q1, lambda_k1, lambda_q2, lambda_k2)]
        a_out = _diff_attn(proj, lam_params, subln_g[l].reshape(HEAD_DIM, 1), batch, seq, n_dh,
                           0, n_dh, 2 * n_dh, lam_init)
        b_out = _moba_attn(proj, batch, seq, n_mh, 3 * n_dh, 3 * n_dh + n_mh, 3 * n_dh + 2 * n_mh)
        h1, h1b = _out_ln1(a_out, b_out, w_out[l].astype(BF16), h,
                           ln1_g[l].reshape(1, d_model), ln1_b[l].reshape(1, d_model))
        act = _ffn_up(h1b, w_up[l], conv_w[l], conv_b[l].reshape(1, -1), seq)
        h = _ffn_down(act, w_down[l].astype(BF16), h1, ln2_g[l].reshape(1, d_model),
                      ln2_b[l].reshape(1, d_model))
    return h.reshape(batch, seq, d_model)
```
